```python
import math
import jax, jax.numpy as jnp
from jax import lax
import numpy as np

D_MODEL = 2048
BATCH = 2
SEQ = 8192
DEPTH = 2

N_MIXERS = 2
N_MLA = (DEPTH + 1) // 2
N_SSD = DEPTH // 2

ALPHA = (2 * DEPTH) ** 0.25
BETA = (8 * DEPTH) ** -0.25

PLE_DIM = 256

MLA_HEADS = 16
Q_LORA = 512
KV_LORA = 512
QK_NOPE = 128
QK_ROPE = 64
V_HEAD = 128
QK_HEAD = QK_NOPE + QK_ROPE
MLA_WIDTH = MLA_HEADS * V_HEAD
MLA_IN_DIM = Q_LORA + KV_LORA + QK_ROPE + MLA_WIDTH
ROPE_BASE = 10000.0
Q_BLOCK = 128
ATTN_SCALE = QK_HEAD ** -0.5

EXPAND = 2
D_INNER = EXPAND * D_MODEL
SSD_HEAD_DIM = 64
SSD_HEADS = D_INNER // SSD_HEAD_DIM
N_GROUPS = 8
HEADS_PER_GROUP = SSD_HEADS // N_GROUPS
D_STATE = 128
D_CONV = 7
CONV_DIM = D_INNER + 2 * N_GROUPS * D_STATE
SSD_IN_DIM = D_INNER + CONV_DIM + 2 * SSD_HEADS
CHUNK = 128

LN_EPS = 1e-5
RMS_EPS = 1e-6

kernel_name = "hybrid_mla_ssd_deepnorm_encoder"


def _layernorm(x, g, b):
    xf = x.astype(jnp.float32)
    mu = jnp.mean(xf, axis=-1, keepdims=True)
    var = jnp.mean(jnp.square(xf - mu), axis=-1, keepdims=True)
    return ((xf - mu) * lax.rsqrt(var + LN_EPS) * g + b).astype(x.dtype)


def _rmsnorm(x, g):
    xf = x.astype(jnp.float32)
    return (xf * lax.rsqrt(jnp.mean(jnp.square(xf), axis=-1, keepdims=True) + RMS_EPS) * g).astype(x.dtype)


def _rope(t, cos, sin):
    t1, t2 = jnp.split(t.astype(jnp.float32), 2, axis=-1)
    return jnp.concatenate([t1 * cos - t2 * sin, t2 * cos + t1 * sin], axis=-1).astype(t.dtype)


def _mla_attention(q_nope, q_rope, k_nope, k_rope, v):
    b, s, h, _ = q_nope.shape
    nb = s // Q_BLOCK

    def block(args):
        qn, qr = args
        sc = (jnp.einsum('bqhd,bkhd->bhqk', qn, k_nope)
              + jnp.einsum('bqhr,bkr->bhqk', qr, k_rope)).astype(jnp.float32) * ATTN_SCALE
        pr = jax.nn.softmax(sc, axis=-1).astype(v.dtype)
        return jnp.einsum('bhqk,bkhd->bqhd', pr, v)

    qn = q_nope.reshape(b, nb, Q_BLOCK, h, QK_NOPE).swapaxes(0, 1)
    qr = q_rope.reshape(b, nb, Q_BLOCK, h, QK_ROPE).swapaxes(0, 1)
    o = lax.map(block, (qn, qr))
    return o.swapaxes(0, 1).reshape(b, s, h * V_HEAD)


def _mla_mixer(x, cos, sin, w_in, q_norm, kv_norm, w_uq, w_ukv, w_o):
    b, s, _ = x.shape
    h = x @ w_in
    c_q = h[..., :Q_LORA]
    c_kv = h[..., Q_LORA:Q_LORA + KV_LORA]
    k_r = h[..., Q_LORA + KV_LORA:Q_LORA + KV_LORA + QK_ROPE]
    z = h[..., Q_LORA + KV_LORA + QK_ROPE:]
    q = (_rmsnorm(c_q, q_norm) @ w_uq).reshape(b, s, MLA_HEADS, QK_HEAD)
    q_nope, q_rope = q[..., :QK_NOPE], _rope(q[..., QK_NOPE:], cos[:, :, None], sin[:, :, None])
    kv = (_rmsnorm(c_kv, kv_norm) @ w_ukv).reshape(b, s, MLA_HEADS, QK_NOPE + V_HEAD)
    k_nope, v = kv[..., :QK_NOPE], kv[..., QK_NOPE:]
    k_rope = _rope(k_r, cos, sin)
    o = _mla_attention(q_nope, q_rope, k_nope, k_rope, v)
    return (o * jax.nn.silu(z)) @ w_o


def _ssd_scan(xs, dt, a_neg, bm, cm):
    b, s = xs.shape[:2]
    nc = s // CHUNK
    G, Hg, P, N = N_GROUPS, HEADS_PER_GROUP, SSD_HEAD_DIM, D_STATE
    X = (xs.astype(jnp.float32) * dt[..., None]).reshape(b, nc, CHUNK, G, Hg, P).swapaxes(0, 1)
    a = (dt * a_neg).reshape(b, nc, CHUNK, G, Hg).swapaxes(0, 1)
    Bc = bm.astype(jnp.float32).reshape(b, nc, CHUNK, G, N).swapaxes(0, 1)
    Cc = cm.astype(jnp.float32).reshape(b, nc, CHUNK, G, N).swapaxes(0, 1)
    mask = jnp.tril(jnp.ones((CHUNK, CHUNK), dtype=bool))[None, :, :, None, None]

    def step(state, inp):
        xc, ac, bc, cc = inp
        acs = jnp.cumsum(ac, axis=1)
        seg = acs[:, :, None] - acs[:, None, :]
        lmat = jnp.exp(jnp.where(mask, seg, -jnp.inf))
        cb = jnp.einsum('bign,bjgn->bijg', cc, bc)
        y = jnp.einsum('bijg,bijgh,bjghp->bighp', cb, lmat, xc)
        y = y + jnp.einsum('bign,bghpn,bigh->bighp', cc, state, jnp.exp(acs))
        decay_end = jnp.exp(acs[:, -1:] - acs)
        state = (state * jnp.exp(acs[:, -1])[..., None, None]
                 + jnp.einsum('bjgn,bjgh,bjghp->bghpn', bc, decay_end, xc))
        return state, y

    state0 = jnp.zeros((b, G, Hg, P, N), jnp.float32)
    _, ys = lax.scan(step, state0, (X, a, Bc, Cc))
    return ys.swapaxes(0, 1).reshape(b, s, SSD_HEADS, P).astype(xs.dtype)


def _ssd_mixer(x, w_in, conv_w, conv_b, dt_bias, a_log, d_skip, norm_w, w_out):
    b, s, _ = x.shape
    h = x @ w_in
    z = h[..., :D_INNER]
    xbc = h[..., D_INNER:D_INNER + CONV_DIM]
    dt = h[..., D_INNER + CONV_DIM:].reshape(b, s, 2, SSD_HEADS)
    pad = D_CONV // 2
    xbc = lax.conv_general_dilated(xbc, conv_w[:, None, :].astype(xbc.dtype), window_strides=(1,),
                                   padding=[(pad, pad)], dimension_numbers=('NWC', 'WIO', 'NWC'),
                                   feature_group_count=CONV_DIM) + conv_b
    xbc = jax.nn.silu(xbc)
    xs = xbc[..., :D_INNER].reshape(b, s, SSD_HEADS, SSD_HEAD_DIM)
    bm = xbc[..., D_INNER:D_INNER + N_GROUPS * D_STATE].reshape(b, s, N_GROUPS, D_STATE)
    cm = xbc[..., D_INNER + N_GROUPS * D_STATE:].reshape(b, s, N_GROUPS, D_STATE)
    dt = jax.nn.softplus(dt.astype(jnp.float32) + dt_bias.astype(jnp.float32))
    a_neg = -jnp.exp(a_log.astype(jnp.float32))
    y_f = _ssd_scan(xs, dt[:, :, 0], a_neg[0], bm, cm)
    flip = lambda t: jnp.flip(t, axis=1)
    y_b = flip(_ssd_scan(flip(xs), flip(dt[:, :, 1]), a_neg[1], flip(bm), flip(cm)))
    y = (y_f + y_b + xs * d_skip[:, None]).reshape(b, s, D_INNER)
    y = _rmsnorm(y * jax.nn.silu(z), norm_w)
    return y @ w_out


def setup_inputs(seed: int = 0) -> dict:
    key = jax.random.key(seed)
    ks = jax.random.split(key, 24)
    nrm = lambda k, shape, scale: jax.random.normal(k, shape, jnp.float32) * scale
    x = nrm(ks[0], (BATCH, SEQ, D_MODEL), 1.0)
    p = nrm(ks[1], (DEPTH, BATCH, SEQ, PLE_DIM), 1.0)
    positions = jnp.broadcast_to(jnp.arange(SEQ, dtype=jnp.int32), (BATCH, SEQ))
    ln_g = 1.0 + nrm(ks[2], (DEPTH, D_MODEL), 0.02)
    ln_b = nrm(ks[3], (DEPTH, D_MODEL), 0.02)
    ple_w_proj = nrm(ks[4], (DEPTH, PLE_DIM, D_MODEL), PLE_DIM ** -0.5)
    ple_w_gate = nrm(ks[5], (DEPTH, D_MODEL, D_MODEL), D_MODEL ** -0.5)
    mla_w_in = nrm(ks[6], (N_MLA, D_MODEL, MLA_IN_DIM), D_MODEL ** -0.5)
    mla_q_norm = 1.0 + nrm(ks[7], (N_MLA, Q_LORA), 0.02)
    mla_kv_norm = 1.0 + nrm(ks[8], (N_MLA, KV_LORA), 0.02)
    mla_w_uq = nrm(ks[9], (N_MLA, Q_LORA, MLA_HEADS * QK_HEAD), Q_LORA ** -0.5)
    kv_scale = jnp.tile(jnp.concatenate([jnp.ones((QK_NOPE,), jnp.float32),
                                         jnp.full((V_HEAD,), BETA, jnp.float32)]), MLA_HEADS)
    mla_w_ukv = nrm(ks[10], (N_MLA, KV_LORA, MLA_HEADS * (QK_NOPE + V_HEAD)), KV_LORA ** -0.5) * kv_scale
    mla_w_o = nrm(ks[11], (N_MLA, MLA_WIDTH, D_MODEL), MLA_WIDTH ** -0.5 * BETA)
    ssd_w_in = nrm(ks[12], (N_SSD, D_MODEL, SSD_IN_DIM), D_MODEL ** -0.5)
    ssd_conv_w = nrm(ks[13], (N_SSD, D_CONV, CONV_DIM), D_CONV ** -0.5)
    ssd_conv_b = nrm(ks[14], (N_SSD, CONV_DIM), 0.02)
    u = jax.random.uniform(ks[15], (N_SSD, 2, SSD_HEADS), jnp.float32)
    dt0 = jnp.exp(u * (math.log(0.1) - math.log(0.001)) + math.log(0.001))
    ssd_dt_bias = dt0 + jnp.log(-jnp.expm1(-dt0))
    ssd_a_log = jnp.log(jax.random.uniform(ks[16], (N_SSD, 2, SSD_HEADS), jnp.float32, 1.0, 16.0))
    ssd_d = 1.0 + nrm(ks[17], (N_SSD, SSD_HEADS), 0.02)
    ssd_norm = 1.0 + nrm(ks[18], (N_SSD, D_INNER), 0.02)
    ssd_w_out = nrm(ks[19], (N_SSD, D_INNER, D_MODEL), D_INNER ** -0.5 * BETA)
    return {"x": x, "p": p, "positions": positions, "ln_g": ln_g, "ln_b": ln_b,
            "ple_w_proj": ple_w_proj, "ple_w_gate": ple_w_gate,
            "mla_w_in": mla_w_in, "mla_q_norm": mla_q_norm, "mla_kv_norm": mla_kv_norm,
            "mla_w_uq": mla_w_uq, "mla_w_ukv": mla_w_ukv, "mla_w_o": mla_w_o,
            "ssd_w_in": ssd_w_in, "ssd_conv_w": ssd_conv_w, "ssd_conv_b": ssd_conv_b,
            "ssd_dt_bias": ssd_dt_bias, "ssd_a_log": ssd_a_log, "ssd_d": ssd_d,
            "ssd_norm": ssd_norm, "ssd_w_out": ssd_w_out}


def reference(x, p, positions, ln_g, ln_b, ple_w_proj, ple_w_gate,
              mla_w_in, mla_q_norm, mla_kv_norm, mla_w_uq, mla_w_ukv, mla_w_o,
              ssd_w_in, ssd_conv_w, ssd_conv_b, ssd_dt_bias, ssd_a_log, ssd_d,
              ssd_norm, ssd_w_out):
    inv_freq = 1.0 / (ROPE_BASE ** (jnp.arange(0, QK_ROPE, 2, dtype=jnp.float32) / QK_ROPE))
    ang = positions.astype(jnp.float32)[..., None] * inv_freq
    cos, sin = jnp.cos(ang), jnp.sin(ang)
    for i in range(DEPTH):
        j = i // N_MIXERS
        if i % N_MIXERS == 0:
            h = _mla_mixer(x, cos, sin, mla_w_in[j], mla_q_norm[j], mla_kv_norm[j],
                           mla_w_uq[j], mla_w_ukv[j], mla_w_o[j])
        else:
            h = _ssd_mixer(x, ssd_w_in[j], ssd_conv_w[j], ssd_conv_b[j], ssd_dt_bias[j],
                           ssd_a_log[j], ssd_d[j], ssd_norm[j], ssd_w_out[j])
        x = _layernorm(ALPHA * x + h, ln_g[i], ln_b[i])
        x = x + (p[i] @ ple_w_proj[i]) * jax.nn.sigmoid(x @ ple_w_gate[i])
    return x
```

```python
import functools

import jax
import jax.numpy as jnp
from jax import lax
from jax.experimental import pallas as pl
from jax.experimental.pallas import tpu as pltpu

F32 = jnp.float32
BF16 = jnp.bfloat16

D_MODEL = 2048
DEPTH = 2
ALPHA = (2 * DEPTH) ** 0.25
PLE_DIM = 256

MLA_HEADS = 16
Q_LORA = 512
KV_LORA = 512
QK_NOPE = 128
QK_ROPE = 64
V_HEAD = 128
QK_HEAD = QK_NOPE + QK_ROPE
MLA_WIDTH = MLA_HEADS * V_HEAD
ROPE_BASE = 10000.0
ATTN_SCALE = QK_HEAD ** -0.5
QK_PAD = 256
ROPE_PAD = 128

D_INNER = 4096
SSD_HEAD_DIM = 64
SSD_HEADS = 64
N_GROUPS = 8
HEADS_PER_GROUP = 8
D_STATE = 128
D_CONV = 7
CONV_PAD = D_CONV // 2
GROUP_WIDTH = HEADS_PER_GROUP * SSD_HEAD_DIM
BC_DIM = N_GROUPS * D_STATE
CONV_DIM = D_INNER + 2 * BC_DIM
CHUNK = 128

LN_EPS = 1e-5
RMS_EPS = 1e-6

MIB = 1024 * 1024


def _params(semantics, vmem_mib):
    return pltpu.CompilerParams(dimension_semantics=semantics, vmem_limit_bytes=vmem_mib * MIB)


def _resident(shape):
    zeros = (0,) * len(shape)
    return pl.BlockSpec(shape, lambda *_: zeros, pipeline_mode=pl.Buffered(1))


def _rmsnorm(v, g):
    return v * lax.rsqrt(jnp.mean(v * v, axis=-1, keepdims=True) + RMS_EPS) * g


def _layernorm(v, g, b):
    mu = jnp.mean(v, axis=-1, keepdims=True)
    d = v - mu
    var = jnp.mean(d * d, axis=-1, keepdims=True)
    return d * lax.rsqrt(var + LN_EPS) * g + b


def _silu(v):
    return v * jax.nn.sigmoid(v)


def _rope(u, ct, sa, sb):
    return u * ct + pltpu.roll(u, QK_ROPE // 2, 1) * sa + pltpu.roll(u, ROPE_PAD - QK_ROPE // 2, 1) * sb


def _rope_table_kernel(pos_ref, freq_ref, ct_ref, sa_ref, sb_ref):
    ang = pos_ref[...] * freq_ref[...]
    c = jnp.cos(ang)
    s = jnp.sin(ang)
    lane = lax.broadcasted_iota(jnp.int32, ang.shape, 1)
    half = QK_ROPE // 2
    ct_ref[...] = jnp.where(lane < QK_ROPE, c, 0.0)
    sa_ref[...] = jnp.where((lane >= half) & (lane < QK_ROPE), s, 0.0)
    sb_ref[...] = jnp.where(lane < half, -s, 0.0)


def _rope_tables(pos_f32, freq_row, tm=2048):
    t = pos_f32.shape[0]
    tab = jax.ShapeDtypeStruct((t, ROPE_PAD), F32)
    row = pl.BlockSpec((tm, ROPE_PAD), lambda i: (i, 0))
    return pl.pallas_call(
        _rope_table_kernel,
        grid=(t // tm,),
        in_specs=[pl.BlockSpec((tm, 1), lambda i: (i, 0)), pl.BlockSpec((1, ROPE_PAD), lambda i: (0, 0))],
        out_specs=[row, row, row],
        out_shape=[tab, tab, tab],
        compiler_params=_params(("parallel",), 32),
        name="rope_tables",
    )(pos_f32, freq_row)


def _mla_in_kernel(x_ref, w_ref, qn_ref, kvn_ref, ct_ref, sa_ref, sb_ref, cq_ref, ckv_ref, kr_ref, g_ref):
    xb = x_ref[...].astype(BF16)
    lat = Q_LORA + KV_LORA + ROPE_PAD
    h = jnp.dot(xb, w_ref[:, :lat], preferred_element_type=F32)
    cq_ref[...] = _rmsnorm(h[:, :Q_LORA], qn_ref[...]).astype(BF16)
    ckv_ref[...] = _rmsnorm(h[:, Q_LORA:Q_LORA + KV_LORA], kvn_ref[...]).astype(BF16)
    kr_ref[...] = _rope(h[:, Q_LORA + KV_LORA:], ct_ref[...], sa_ref[...], sb_ref[...]).astype(BF16)
    z = jnp.dot(xb, w_ref[:, lat:], preferred_element_type=F32)
    g_ref[...] = _silu(z).astype(BF16)


def _mla_in(x2, w_in_p, qn, kvn, ct, sa, sb, tm=512):
    t = x2.shape[0]
    n = w_in_p.shape[1]
    row = lambda w: pl.BlockSpec((tm, w), lambda i: (i, 0))
    return pl.pallas_call(
        _mla_in_kernel,
        grid=(t // tm,),
        in_specs=[row(D_MODEL), _resident((D_MODEL, n)), _resident((1, Q_LORA)), _resident((1, KV_LORA)),
                  row(ROPE_PAD), row(ROPE_PAD), row(ROPE_PAD)],
        out_specs=[row(Q_LORA), row(KV_LORA), row(ROPE_PAD), row(MLA_WIDTH)],
        out_shape=[jax.ShapeDtypeStruct((t, Q_LORA), BF16), jax.ShapeDtypeStruct((t, KV_LORA), BF16),
                   jax.ShapeDtypeStruct((t, ROPE_PAD), BF16), jax.ShapeDtypeStruct((t, MLA_WIDTH), BF16)],
        compiler_params=_params(("parallel",), 48),
        name="mla_in",
    )(x2, w_in_p, qn, kvn, ct, sa, sb)


def _mla_up_kernel(cq_ref, ckv_ref, kr_ref, ct_ref, sa_ref, sb_ref, wuq_ref, wukv_ref, q_ref, k_ref, v_ref):
    q = jnp.dot(cq_ref[...], wuq_ref[0], preferred_element_type=F32) * ATTN_SCALE
    q_ref[0, 0, :, :QK_NOPE] = q[:, :QK_NOPE].astype(BF16)
    q_ref[0, 0, :, QK_NOPE:] = _rope(q[:, QK_NOPE:], ct_ref[...], sa_ref[...], sb_ref[...]).astype(BF16)
    kv = jnp.dot(ckv_ref[...], wukv_ref[0], preferred_element_type=F32)
    k_ref[0, 0, :, :QK_NOPE] = kv[:, :QK_NOPE].astype(BF16)
    k_ref[0, 0, :, QK_NOPE:] = kr_ref[...]
    v_ref[0, 0] = kv[:, QK_NOPE:].astype(BF16)


def _mla_up(cqn, ckvn, kr, ct, sa, sb, wuq_p, wukv_p, batch, seq, tm=1024):
    nt = seq // tm
    row = lambda w: pl.BlockSpec((tm, w), lambda b, i, h: (b * nt + i, 0))
    wspec = pl.BlockSpec((1, Q_LORA, QK_PAD), lambda b, i, h: (h, 0, 0))
    head = lambda w: pl.BlockSpec((1, 1, tm, w), lambda b, i, h: (b, h, i, 0))
    return pl.pallas_call(
        _mla_up_kernel,
        grid=(batch, nt, MLA_HEADS),
        in_specs=[row(Q_LORA), row(KV_LORA), row(ROPE_PAD), row(ROPE_PAD), row(ROPE_PAD), row(ROPE_PAD),
                  wspec, wspec],
        out_specs=[head(QK_PAD), head(QK_PAD), head(V_HEAD)],
        out_shape=[jax.ShapeDtypeStruct((batch, MLA_HEADS, seq, QK_PAD), BF16),
                   jax.ShapeDtypeStruct((batch, MLA_HEADS, seq, QK_PAD), BF16),
                   jax.ShapeDtypeStruct((batch, MLA_HEADS, seq, V_HEAD), BF16)],
        compiler_params=_params(("parallel", "parallel", "arbitrary"), 32),
        name="mla_up",
    )(cqn, ckvn, kr, ct, sa, sb, wuq_p, wukv_p)


def _attn_kernel(q_ref, k_ref, v_ref, o_ref, *, tk):
    q = q_ref[0, 0]
    tq = q.shape[0]
    n_chunks = k_ref.shape[2] // tk

    def body(c, carry):
        m, l, acc = carry
        start = pl.multiple_of(c * tk, tk)
        k = k_ref[0, 0, pl.ds(start, tk), :]
        v = v_ref[0, 0, pl.ds(start, tk), :]
        s = lax.dot_general(q, k, (((1,), (1,)), ((), ())), preferred_element_type=F32)
        m_new = jnp.maximum(m, jnp.max(s, axis=-1, keepdims=True))
        alpha = jnp.exp(m - m_new)
        p = jnp.exp(s - m_new)
        l = alpha * l + jnp.sum(p, axis=-1, keepdims=True)
        acc = alpha * acc + jnp.dot(p.astype(BF16), v, preferred_element_type=F32)
        return m_new, l, acc

    init = (jnp.full((tq, 1), -jnp.inf, F32), jnp.zeros((tq, 1), F32), jnp.zeros((tq, V_HEAD), F32))
    _, l, acc = lax.fori_loop(0, n_chunks, body, init)
    o_ref[0] = (acc / l).astype(o_ref.dtype)


def _attention(q, k, v, tq=512, tk=1024):
    batch, heads, seq, _ = q.shape
    return pl.pallas_call(
        functools.partial(_attn_kernel, tk=tk),
        grid=(batch, heads, seq // tq),
        in_specs=[pl.BlockSpec((1, 1, tq, QK_PAD), lambda b, h, i: (b, h, i, 0)),
                  pl.BlockSpec((1, 1, seq, QK_PAD), lambda b, h, i: (b, h, 0, 0)),
                  pl.BlockSpec((1, 1, seq, V_HEAD), lambda b, h, i: (b, h, 0, 0))],
        out_specs=pl.BlockSpec((1, tq, V_HEAD), lambda b, h, i: (b, i, h)),
        out_shape=jax.ShapeDtypeStruct((batch, seq, heads * V_HEAD), BF16),
        compiler_params=_params(("parallel", "parallel", "arbitrary"), 48),
        name="mla_attention",
    )(q, k, v)


def _deepnorm_ple(x, h, p, lng, lnb, wpe, wpg):
    y = _layernorm(ALPHA * x + h, lng, lnb)
    emb = jnp.dot(p.astype(BF16), wpe, preferred_element_type=F32)
    gate = jax.nn.sigmoid(jnp.dot(y.astype(BF16), wpg, preferred_element_type=F32))
    return y + emb * gate


def _mla_out_kernel(o_ref, g_ref, x_ref, p_ref, wo_ref, lng_ref, lnb_ref, wpe_ref, wpg_ref, out_ref):
    og = (o_ref[...].astype(F32) * g_ref[...].astype(F32)).astype(BF16)
    h = jnp.dot(og, wo_ref[...], preferred_element_type=F32)
    out_ref[...] = _deepnorm_ple(x_ref[...], h, p_ref[...], lng_ref[...], lnb_ref[...], wpe_ref[...], wpg_ref[...])


def _mla_out(o, g, x2, p0, wo, lng, lnb, wpe, wpg, tm=512):
    t = x2.shape[0]
    row = lambda w: pl.BlockSpec((tm, w), lambda i: (i, 0))
    return pl.pallas_call(
        _mla_out_kernel,
        grid=(t // tm,),
        in_specs=[row(MLA_WIDTH), row(MLA_WIDTH), row(D_MODEL), row(PLE_DIM),
                  _resident((MLA_WIDTH, D_MODEL)), _resident((1, D_MODEL)), _resident((1, D_MODEL)),
                  _resident((PLE_DIM, D_MODEL)), _resident((D_MODEL, D_MODEL))],
        out_specs=row(D_MODEL),
        out_shape=jax.ShapeDtypeStruct((t, D_MODEL), F32),
        compiler_params=_params(("parallel",), 56),
        name="mla_out",
    )(o, g, x2, p0, wo, lng, lnb, wpe, wpg)


def _proj_kernel(x_ref, w_ref, o_ref, xb_ref):
    @pl.when(pl.program_id(1) == 0)
    def _():
        xb_ref[...] = x_ref[...].astype(BF16)

    o_ref[...] = jnp.dot(xb_ref[...], w_ref[...], preferred_element_type=F32).astype(o_ref.dtype)


def _proj(x2, w, out_dtype, tm=1024, tn=512):
    t, kdim = x2.shape
    n = w.shape[1]
    return pl.pallas_call(
        _proj_kernel,
        grid=(t // tm, n // tn),
        in_specs=[pl.BlockSpec((tm, kdim), lambda i, j: (i, 0)), pl.BlockSpec((kdim, tn), lambda i, j: (0, j))],
        out_specs=pl.BlockSpec((tm, tn), lambda i, j: (i, j)),
        out_shape=jax.ShapeDtypeStruct((t, n), out_dtype),
        scratch_shapes=[pltpu.VMEM((tm, kdim), BF16)],
        compiler_params=_params(("parallel", "arbitrary"), 48),
        name="ssd_in_proj",
    )(x2, w)


def _dt_kernel(x_ref, w_ref, bias_ref, o_ref):
    h = jnp.dot(x_ref[...].astype(BF16), w_ref[...], preferred_element_type=F32) + bias_ref[...]
    o_ref[...] = jnp.maximum(h, 0.0) + jnp.log1p(jnp.exp(-jnp.abs(h)))


def _dt_proj(x2, w_dt, bias_row, tm=1024):
    t, kdim = x2.shape
    n = w_dt.shape[1]
    return pl.pallas_call(
        _dt_kernel,
        grid=(t // tm,),
        in_specs=[pl.BlockSpec((tm, kdim), lambda i: (i, 0)), _resident((kdim, n)), _resident((1, n))],
        out_specs=pl.BlockSpec((tm, n), lambda i: (i, 0)),
        out_shape=jax.ShapeDtypeStruct((t, n), F32),
        compiler_params=_params(("parallel",), 32),
        name="ssd_dt_proj",
    )(x2, w_dt, bias_row)


CONV_HALO = 16
CONV_LANES = 512
CONV_ROWS = 32


def _conv_kernel(prev_ref, cur_ref, next_ref, w_ref, b_ref, xs_ref, bm_ref, cm_ref, ext_ref):
    i = pl.program_id(1)
    tc = cur_ref.shape[0]
    has_prev = (i > 0).astype(F32)
    has_next = (i < pl.num_programs(1) - 1).astype(F32)
    ext_ref[0:CONV_HALO, :] = prev_ref[...].astype(F32) * has_prev
    ext_ref[CONV_HALO:CONV_HALO + tc, :] = cur_ref[...].astype(F32)
    ext_ref[CONV_HALO + tc:, :] = next_ref[...].astype(F32) * has_next
    for cb in range(CONV_DIM // CONV_LANES):
        lanes = slice(cb * CONV_LANES, (cb + 1) * CONV_LANES)
        taps = [w_ref[k:k + 1, lanes] for k in range(D_CONV)]
        bias = b_ref[:, lanes]
        for rb in range(tc // CONV_ROWS):
            base = CONV_HALO - CONV_PAD + rb * CONV_ROWS
            acc = bias + ext_ref[base:base + CONV_ROWS, lanes] * taps[0]
            for k in range(1, D_CONV):
                acc = acc + ext_ref[base + k:base + k + CONV_ROWS, lanes] * taps[k]
            out = _silu(acc).astype(BF16)
            rows = slice(rb * CONV_ROWS, (rb + 1) * CONV_ROWS)
            if cb < D_INNER // CONV_LANES:
                xs_ref[rows, lanes] = out
            elif cb < (D_INNER + BC_DIM) // CONV_LANES:
                off = cb * CONV_LANES - D_INNER
                bm_ref[rows, off:off + CONV_LANES] = out
            else:
                off = cb * CONV_LANES - D_INNER - BC_DIM
                cm_ref[rows, off:off + CONV_LANES] = out


def _conv(xbc, conv_w, conv_b, batch, seq, tc=256):
    t = xbc.shape[0]
    nt = seq // tc
    per = tc // CONV_HALO
    last_halo = t // CONV_HALO - 1
    row = lambda w: pl.BlockSpec((tc, w), lambda b, i: (b * nt + i, 0))
    return pl.pallas_call(
        _conv_kernel,
        grid=(batch, nt),
        in_specs=[pl.BlockSpec((CONV_HALO, CONV_DIM), lambda b, i: (jnp.maximum((b * nt + i) * per - 1, 0), 0)),
                  row(CONV_DIM),
                  pl.BlockSpec((CONV_HALO, CONV_DIM), lambda b, i: (jnp.minimum((b * nt + i + 1) * per, last_halo), 0)),
                  _resident((D_CONV, CONV_DIM)), _resident((1, CONV_DIM))],
        out_specs=[row(D_INNER), row(BC_DIM), row(BC_DIM)],
        out_shape=[jax.ShapeDtypeStruct((t, D_INNER), BF16), jax.ShapeDtypeStruct((t, BC_DIM), BF16),
                   jax.ShapeDtypeStruct((t, BC_DIM), BF16)],
        scratch_shapes=[pltpu.VMEM((tc + 2 * CONV_HALO, CONV_DIM), F32)],
        compiler_params=_params(("parallel", "parallel"), 48),
        name="ssd_conv",
    )(xbc, xbc, xbc, conv_w, conv_b)


def _split3(v):
    hi = v.astype(BF16)
    r = v - hi.astype(F32)
    mid = r.astype(BF16)
    lo = (r - mid.astype(F32)).astype(BF16)
    return hi, mid, lo


def _expand_heads(v, e):
    hi = v.astype(BF16)
    lo = (v - hi.astype(F32)).astype(BF16)
    return jnp.dot(hi, e, preferred_element_type=F32) + jnp.dot(lo, e, preferred_element_type=F32)


def _scan_kernel(xs_ref, bm_ref, cm_ref, dt_ref, alog_ref, e_ref, y_ref, state_ref):
    d = pl.program_id(1)
    c = pl.program_id(2)

    @pl.when(c == 0)
    def _():
        state_ref[...] = jnp.zeros_like(state_ref)

    row = lax.broadcasted_iota(jnp.int32, (CHUNK, CHUNK), 0)
    col = lax.broadcasted_iota(jnp.int32, (CHUNK, CHUNK), 1)
    mask = jnp.where(d == 0, row - col, col - row) >= 0
    cum = mask.astype(BF16)

    dt = dt_ref[0]
    a = dt * -jnp.exp(alog_ref[0])
    a_hi, a_mid, a_lo = _split3(a)
    acs = (jnp.dot(cum, a_hi, preferred_element_type=F32) + jnp.dot(cum, a_mid, preferred_element_type=F32)
           + jnp.dot(cum, a_lo, preferred_element_type=F32))
    tot = jnp.sum(a, axis=0, keepdims=True)
    acs_t = acs.T
    dt_t = dt.T

    e = e_ref[...]
    decay_in = _expand_heads(jnp.exp(acs), e)
    decay_out = _expand_heads(jnp.exp(tot - acs) * dt, e)
    decay_tot = _expand_heads(jnp.broadcast_to(jnp.exp(tot), (8, SSD_HEADS)), e)[0:1]

    for g in range(N_GROUPS):
        gl = slice(g * GROUP_WIDTH, (g + 1) * GROUP_WIDTH)
        bg = bm_ref[:, g * D_STATE:(g + 1) * D_STATE]
        cg = cm_ref[:, g * D_STATE:(g + 1) * D_STATE]
        cb = lax.dot_general(cg, bg, (((1,), (1,)), ((), ())), preferred_element_type=F32)
        state_g = state_ref[:, gl]
        y_state = jnp.dot(cg, state_g.astype(BF16), preferred_element_type=F32) * decay_in[:, gl]
        ys = []
        for hh in range(HEADS_PER_GROUP):
            h = g * HEADS_PER_GROUP + hh
            seg = acs[:, h:h + 1] - acs_t[h:h + 1, :]
            lmat = jnp.exp(jnp.where(mask, seg, -jnp.inf))
            m = (cb * lmat * dt_t[h:h + 1, :]).astype(BF16)
            xh = xs_ref[:, h * SSD_HEAD_DIM:(h + 1) * SSD_HEAD_DIM]
            ys.append(jnp.dot(m, xh, preferred_element_type=F32))
        y_ref[0, :, gl] = (jnp.concatenate(ys, axis=1) + y_state).astype(y_ref.dtype)
        xw = (xs_ref[:, gl].astype(F32) * decay_out[:, gl]).astype(BF16)
        bg_t = bg.astype(F32).T.astype(BF16)
        state_ref[:, gl] = state_g * decay_tot[:, gl] + jnp.dot(bg_t, xw, preferred_element_type=F32)


def _scan(xs, bm, cm, dt2, alog, batch, seq):
    t = xs.shape[0]
    nc = seq // CHUNK
    heads = jnp.arange(D_INNER, dtype=jnp.int32) // SSD_HEAD_DIM
    expand = (heads[None, :] == jnp.arange(SSD_HEADS, dtype=jnp.int32)[:, None]).astype(BF16)

    def chunk(b, d, c):
        return b * nc + jnp.where(d == 0, c, nc - 1 - c)

    row = lambda w: pl.BlockSpec((CHUNK, w), lambda b, d, c: (chunk(b, d, c), 0))
    return pl.pallas_call(
        _scan_kernel,
        grid=(batch, 2, nc),
        in_specs=[row(D_INNER), row(BC_DIM), row(BC_DIM),
                  pl.BlockSpec((1, CHUNK, SSD_HEADS), lambda b, d, c: (d, chunk(b, d, c), 0)),
                  pl.BlockSpec((1, 1, SSD_HEADS), lambda b, d, c: (d, 0, 0)),
                  pl.BlockSpec((SSD_HEADS, D_INNER), lambda b, d, c: (0, 0))],
        out_specs=pl.BlockSpec((1, CHUNK, D_INNER), lambda b, d, c: (d, chunk(b, d, c), 0)),
        out_shape=jax.ShapeDtypeStruct((2, t, D_INNER), BF16),
        scratch_shapes=[pltpu.VMEM((D_STATE, D_INNER), F32)],
        compiler_params=_params(("parallel", "arbitrary", "arbitrary"), 48),
        name="ssd_scan",
    )(xs, bm, cm, dt2, alog, expand)


def _ssd_out_kernel(yf_ref, yb_ref, xs_ref, z_ref, x_ref, p_ref, dskip_ref, nw_ref, wout_ref,
                    lng_ref, lnb_ref, wpe_ref, wpg_ref, out_ref):
    y = yf_ref[0].astype(F32) + yb_ref[0].astype(F32) + xs_ref[...].astype(F32) * dskip_ref[...]
    y = _rmsnorm(y * _silu(z_ref[...].astype(F32)), nw_ref[...])
    h = jnp.dot(y.astype(BF16), wout_ref[...], preferred_element_type=F32)
    out_ref[...] = _deepnorm_ple(x_ref[...], h, p_ref[...], lng_ref[...], lnb_ref[...], wpe_ref[...], wpg_ref[...])


def _ssd_out(y2, xs, z, x2, p1, dskip, nw, wout, lng, lnb, wpe, wpg, tm=256):
    t = x2.shape[0]
    row = lambda w: pl.BlockSpec((tm, w), lambda i: (i, 0))
    ydir = lambda d: pl.BlockSpec((1, tm, D_INNER), lambda i: (d, i, 0))
    return pl.pallas_call(
        _ssd_out_kernel,
        grid=(t // tm,),
        in_specs=[ydir(0), ydir(1), row(D_INNER), row(D_INNER), row(D_MODEL), row(PLE_DIM),
                  _resident((1, D_INNER)), _resident((1, D_INNER)), _resident((D_INNER, D_MODEL)),
                  _resident((1, D_MODEL)), _resident((1, D_MODEL)),
                  _resident((PLE_DIM, D_MODEL)), _resident((D_MODEL, D_MODEL))],
        out_specs=row(D_MODEL),
        out_shape=jax.ShapeDtypeStruct((t, D_MODEL), F32),
        compiler_params=_params(("parallel",), 56),
        name="ssd_out",
    )(y2, y2, xs, z, x2, p1, dskip, nw, wout, lng, lnb, wpe, wpg)


def kernel(x, p, positions, ln_g, ln_b, ple_w_proj, ple_w_gate, mla_w_in, mla_q_norm, mla_kv_norm, mla_w_uq,
           mla_w_ukv, mla_w_o, ssd_w_in, ssd_conv_w, ssd_conv_b, ssd_dt_bias, ssd_a_log, ssd_d, ssd_norm, ssd_w_out):
    batch, seq, _ = x.shape
    t = batch * seq
    x2 = x.reshape(t, D_MODEL)
    p2 = p.reshape(DEPTH, t, PLE_DIM)
    row = lambda v: v.reshape(1, -1)

    inv_freq = 1.0 / (ROPE_BASE ** (jnp.arange(0, QK_ROPE, 2, dtype=F32) / QK_ROPE))
    freq_row = jnp.concatenate([inv_freq, inv_freq, jnp.zeros((ROPE_PAD - QK_ROPE,), F32)]).reshape(1, ROPE_PAD)
    ct, sa, sb = _rope_tables(positions.astype(F32).reshape(t, 1), freq_row)

    lat = Q_LORA + KV_LORA
    w_in = mla_w_in[0]
    w_in_p = jnp.concatenate([w_in[:, :lat + QK_ROPE], jnp.zeros((D_MODEL, ROPE_PAD - QK_ROPE), F32),
                              w_in[:, lat + QK_ROPE:]], axis=1).astype(BF16)
    wuq_p = jnp.pad(mla_w_uq[0].reshape(Q_LORA, MLA_HEADS, QK_HEAD).transpose(1, 0, 2),
                    ((0, 0), (0, 0), (0, QK_PAD - QK_HEAD))).astype(BF16)
    wukv_p = mla_w_ukv[0].reshape(KV_LORA, MLA_HEADS, QK_NOPE + V_HEAD).transpose(1, 0, 2).astype(BF16)

    cqn, ckvn, kr, gate = _mla_in(x2, w_in_p, row(mla_q_norm[0]), row(mla_kv_norm[0]), ct, sa, sb)
    q, k, v = _mla_up(cqn, ckvn, kr, ct, sa, sb, wuq_p, wukv_p, batch, seq)
    o = _attention(q, k, v).reshape(t, MLA_WIDTH)
    x2 = _mla_out(o, gate, x2, p2[0], mla_w_o[0].astype(BF16), row(ln_g[0]), row(ln_b[0]),
                  ple_w_proj[0].astype(BF16), ple_w_gate[0].astype(BF16))

    w_ssd = ssd_w_in[0].astype(BF16)
    z = _proj(x2, w_ssd[:, :D_INNER], BF16)
    xbc = _proj(x2, w_ssd[:, D_INNER:D_INNER + CONV_DIM], BF16)
    dt = _dt_proj(x2, w_ssd[:, D_INNER + CONV_DIM:], ssd_dt_bias[0].reshape(1, 2 * SSD_HEADS))
    xs, bm, cm = _conv(xbc, ssd_conv_w[0], row(ssd_conv_b[0]), batch, seq)
    dt2 = dt.reshape(t, 2, SSD_HEADS).transpose(1, 0, 2)
    y2 = _scan(xs, bm, cm, dt2, ssd_a_log[0].reshape(2, 1, SSD_HEADS), batch, seq)
    dskip = jnp.repeat(ssd_d[0], SSD_HEAD_DIM).reshape(1, D_INNER)
    x2 = _ssd_out(y2, xs, z, x2, p2[1], dskip, row(ssd_norm[0]), ssd_w_out[0].astype(BF16),
                  row(ln_g[1]), row(ln_b[1]), ple_w_proj[1].astype(BF16), ple_w_gate[1].astype(BF16))
    return x2.reshape(batch, seq, D_MODEL)
```

```python
import functools

import jax
import jax.numpy as jnp
from jax import lax
from jax.experimental import pallas as pl
from jax.experimental.pallas import tpu as pltpu

F32 = jnp.float32
BF16 = jnp.bfloat16

D_MODEL = 2048
DEPTH = 2
ALPHA = (2 * DEPTH) ** 0.25
PLE_DIM = 256

MLA_HEADS = 16
Q_LORA = 512
KV_LORA = 512
QK_NOPE = 128
QK_ROPE = 64
V_HEAD = 128
QK_HEAD = QK_NOPE + QK_ROPE
MLA_WIDTH = MLA_HEADS * V_HEAD
ROPE_BASE = 10000.0
ATTN_SCALE = QK_HEAD ** -0.5
LOG2_E = 1.4426950408889634
QK_PAD = 256
ROPE_PAD = 128

D_INNER = 4096
SSD_HEAD_DIM = 64
SSD_HEADS = 64
N_GROUPS = 8
HEADS_PER_GROUP = 8
D_STATE = 128
D_CONV = 7
CONV_PAD = D_CONV // 2
GROUP_WIDTH = HEADS_PER_GROUP * SSD_HEAD_DIM
BC_DIM = N_GROUPS * D_STATE
CONV_DIM = D_INNER + 2 * BC_DIM
CHUNK = 128

LN_EPS = 1e-5
RMS_EPS = 1e-6

MIB = 1024 * 1024


def _params(semantics, vmem_mib):
    return pltpu.CompilerParams(dimension_semantics=semantics, vmem_limit_bytes=vmem_mib * MIB)


def _resident(shape):
    zeros = (0,) * len(shape)
    return pl.BlockSpec(shape, lambda *_: zeros, pipeline_mode=pl.Buffered(1))


def _rmsnorm(v, g):
    return v * lax.rsqrt(jnp.mean(v * v, axis=-1, keepdims=True) + RMS_EPS) * g


def _layernorm(v, g, b):
    mu = jnp.mean(v, axis=-1, keepdims=True)
    d = v - mu
    var = jnp.mean(d * d, axis=-1, keepdims=True)
    return d * lax.rsqrt(var + LN_EPS) * g + b


def _silu(v):
    return v * jax.nn.sigmoid(v)


def _rope(u, ct, sa, sb):
    return u * ct + pltpu.roll(u, QK_ROPE // 2, 1) * sa + pltpu.roll(u, ROPE_PAD - QK_ROPE // 2, 1) * sb


def _rope_table_kernel(pos_ref, freq_ref, ct_ref, sa_ref, sb_ref):
    ang = pos_ref[...] * freq_ref[...]
    c = jnp.cos(ang)
    s = jnp.sin(ang)
    lane = lax.broadcasted_iota(jnp.int32, ang.shape, 1)
    half = QK_ROPE // 2
    ct_ref[...] = jnp.where(lane < QK_ROPE, c, 0.0)
    sa_ref[...] = jnp.where((lane >= half) & (lane < QK_ROPE), s, 0.0)
    sb_ref[...] = jnp.where(lane < half, -s, 0.0)


def _rope_tables(pos_f32, freq_row, tm=2048):
    t = pos_f32.shape[0]
    tab = jax.ShapeDtypeStruct((t, ROPE_PAD), F32)
    row = pl.BlockSpec((tm, ROPE_PAD), lambda i: (i, 0))
    return pl.pallas_call(
        _rope_table_kernel,
        grid=(t // tm,),
        in_specs=[pl.BlockSpec((tm, 1), lambda i: (i, 0)), pl.BlockSpec((1, ROPE_PAD), lambda i: (0, 0))],
        out_specs=[row, row, row],
        out_shape=[tab, tab, tab],
        compiler_params=_params(("parallel",), 32),
        name="rope_tables",
    )(pos_f32, freq_row)


def _mla_in_kernel(x_ref, w_ref, qn_ref, kvn_ref, ct_ref, sa_ref, sb_ref, cq_ref, ckv_ref, kr_ref, g_ref):
    xb = x_ref[...].astype(BF16)
    lat = Q_LORA + KV_LORA + ROPE_PAD
    h = jnp.dot(xb, w_ref[:, :lat], preferred_element_type=F32)
    cq_ref[...] = _rmsnorm(h[:, :Q_LORA], qn_ref[...]).astype(BF16)
    ckv_ref[...] = _rmsnorm(h[:, Q_LORA:Q_LORA + KV_LORA], kvn_ref[...]).astype(BF16)
    kr_ref[...] = _rope(h[:, Q_LORA + KV_LORA:], ct_ref[...], sa_ref[...], sb_ref[...]).astype(BF16)
    z = jnp.dot(xb, w_ref[:, lat:], preferred_element_type=F32)
    g_ref[...] = _silu(z).astype(BF16)


def _mla_in(x2, w_in_p, qn, kvn, ct, sa, sb, tm=512):
    t = x2.shape[0]
    n = w_in_p.shape[1]
    row = lambda w: pl.BlockSpec((tm, w), lambda i: (i, 0))
    return pl.pallas_call(
        _mla_in_kernel,
        grid=(t // tm,),
        in_specs=[row(D_MODEL), _resident((D_MODEL, n)), _resident((1, Q_LORA)), _resident((1, KV_LORA)),
                  row(ROPE_PAD), row(ROPE_PAD), row(ROPE_PAD)],
        out_specs=[row(Q_LORA), row(KV_LORA), row(ROPE_PAD), row(MLA_WIDTH)],
        out_shape=[jax.ShapeDtypeStruct((t, Q_LORA), BF16), jax.ShapeDtypeStruct((t, KV_LORA), BF16),
                   jax.ShapeDtypeStruct((t, ROPE_PAD), BF16), jax.ShapeDtypeStruct((t, MLA_WIDTH), BF16)],
        compiler_params=_params(("parallel",), 48),
        name="mla_in",
    )(x2, w_in_p, qn, kvn, ct, sa, sb)


def _mla_up_kernel(cq_ref, ckv_ref, kr_ref, ct_ref, sa_ref, sb_ref, wuq_ref, wukv_ref, qt_ref, k_ref, vt_ref):
    q = jnp.dot(cq_ref[...], wuq_ref[0], preferred_element_type=F32) * (ATTN_SCALE * LOG2_E)
    qt_ref[0, 0, :QK_NOPE, :] = q[:, :QK_NOPE].T.astype(BF16)
    qt_ref[0, 0, QK_NOPE:, :] = _rope(q[:, QK_NOPE:], ct_ref[...], sa_ref[...], sb_ref[...]).T.astype(BF16)
    kv = jnp.dot(ckv_ref[...], wukv_ref[0], preferred_element_type=F32)
    k_ref[0, 0, :, :QK_NOPE] = kv[:, :QK_NOPE].astype(BF16)
    k_ref[0, 0, :, QK_NOPE:] = kr_ref[...]
    vt_ref[0, 0] = kv[:, QK_NOPE:].T.astype(BF16)


def _mla_up(cqn, ckvn, kr, ct, sa, sb, wuq_p, wukv_p, batch, seq, tm=1024):
    nt = seq // tm
    row = lambda w: pl.BlockSpec((tm, w), lambda b, i, h: (b * nt + i, 0))
    wspec = pl.BlockSpec((1, Q_LORA, QK_PAD), lambda b, i, h: (h, 0, 0))
    feat = lambda w: pl.BlockSpec((1, 1, w, tm), lambda b, i, h: (b, h, 0, i))
    return pl.pallas_call(
        _mla_up_kernel,
        grid=(batch, nt, MLA_HEADS),
        in_specs=[row(Q_LORA), row(KV_LORA), row(ROPE_PAD), row(ROPE_PAD), row(ROPE_PAD), row(ROPE_PAD),
                  wspec, wspec],
        out_specs=[feat(QK_PAD), pl.BlockSpec((1, 1, tm, QK_PAD), lambda b, i, h: (b, h, i, 0)), feat(V_HEAD)],
        out_shape=[jax.ShapeDtypeStruct((batch, MLA_HEADS, QK_PAD, seq), BF16),
                   jax.ShapeDtypeStruct((batch, MLA_HEADS, seq, QK_PAD), BF16),
                   jax.ShapeDtypeStruct((batch, MLA_HEADS, V_HEAD, seq), BF16)],
        compiler_params=_params(("parallel", "parallel", "arbitrary"), 32),
        name="mla_up",
    )(cqn, ckvn, kr, ct, sa, sb, wuq_p, wukv_p)


def _attn_kernel(qt_ref, k_ref, vt_ref, o_ref, s_ref, *, tk):
    tq = qt_ref.shape[3]
    n_chunks = k_ref.shape[2] // tk
    qt = qt_ref[0, 0]

    def scores(c, slot):
        start = c * tk
        s = jnp.dot(k_ref[0, 0, pl.ds(start, tk), :], qt, preferred_element_type=F32)
        s_ref[slot] = s
        return jnp.max(s, axis=0, keepdims=True)

    def consume(c, slot, m, m_chunk, l, acc):
        m_new = jnp.maximum(m, m_chunk)
        alpha = jnp.exp2(m - m_new)
        p = jnp.exp2(s_ref[slot] - m_new)
        l = alpha * l + jnp.sum(p.reshape(tk // 8, 8, tq), axis=0)
        start = c * tk
        pv = jnp.dot(vt_ref[0, 0, :, pl.ds(start, tk)], p.astype(BF16), preferred_element_type=F32)
        return m_new, l, alpha * acc + pv

    def step(c, slot, carry, prefetch=True):
        m, m_chunk, l, acc = carry
        m_next = scores(c + 1, 1 - slot) if prefetch else m_chunk
        m, l, acc = consume(c, slot, m, m_chunk, l, acc)
        return m, m_next, l, acc

    carry = (jnp.full((1, tq), -jnp.inf, F32), scores(0, 0), jnp.zeros((8, tq), F32), jnp.zeros((V_HEAD, tq), F32))
    for c in range(n_chunks):
        carry = step(c, c % 2, carry, prefetch=c + 1 < n_chunks)
    _, _, l, acc = carry
    o_ref[0] = (acc / jnp.sum(l, axis=0, keepdims=True)).T.astype(o_ref.dtype)


def _attention(qt, k, vt, tq=1024, tk=512):
    batch, heads, seq, _ = k.shape
    assert seq % (2 * tk) == 0 and seq % tq == 0
    return pl.pallas_call(
        functools.partial(_attn_kernel, tk=tk),
        grid=(batch, heads, seq // tq),
        in_specs=[pl.BlockSpec((1, 1, QK_PAD, tq), lambda b, h, i: (b, h, 0, i)),
                  pl.BlockSpec((1, 1, seq, QK_PAD), lambda b, h, i: (b, h, 0, 0)),
                  pl.BlockSpec((1, 1, V_HEAD, seq), lambda b, h, i: (b, h, 0, 0))],
        out_specs=pl.BlockSpec((1, tq, V_HEAD), lambda b, h, i: (b, i, h)),
        out_shape=jax.ShapeDtypeStruct((batch, seq, heads * V_HEAD), BF16),
        scratch_shapes=[pltpu.VMEM((2, tk, tq), F32)],
        compiler_params=_params(("parallel", "parallel", "arbitrary"), 48),
        name="mla_attention",
    )(qt, k, vt)


def _deepnorm_ple(x, h, p, lng, lnb, wpe, wpg):
    y = _layernorm(ALPHA * x + h, lng, lnb)
    emb = jnp.dot(p.astype(BF16), wpe, preferred_element_type=F32)
    gate = jax.nn.sigmoid(jnp.dot(y.astype(BF16), wpg, preferred_element_type=F32))
    return y + emb * gate


def _mla_out_kernel(o_ref, g_ref, x_ref, p_ref, wo_ref, lng_ref, lnb_ref, wpe_ref, wpg_ref, out_ref):
    og = (o_ref[...].astype(F32) * g_ref[...].astype(F32)).astype(BF16)
    h = jnp.dot(og, wo_ref[...], preferred_element_type=F32)
    out_ref[...] = _deepnorm_ple(x_ref[...], h, p_ref[...], lng_ref[...], lnb_ref[...], wpe_ref[...], wpg_ref[...])


def _mla_out(o, g, x2, p0, wo, lng, lnb, wpe, wpg, tm=512):
    t = x2.shape[0]
    row = lambda w: pl.BlockSpec((tm, w), lambda i: (i, 0))
    return pl.pallas_call(
        _mla_out_kernel,
        grid=(t // tm,),
        in_specs=[row(MLA_WIDTH), row(MLA_WIDTH), row(D_MODEL), row(PLE_DIM),
                  _resident((MLA_WIDTH, D_MODEL)), _resident((1, D_MODEL)), _resident((1, D_MODEL)),
                  _resident((PLE_DIM, D_MODEL)), _resident((D_MODEL, D_MODEL))],
        out_specs=row(D_MODEL),
        out_shape=jax.ShapeDtypeStruct((t, D_MODEL), F32),
        compiler_params=_params(("parallel",), 56),
        name="mla_out",
    )(o, g, x2, p0, wo, lng, lnb, wpe, wpg)


def _proj_kernel(x_ref, w_ref, o_ref, xb_ref):
    @pl.when(pl.program_id(1) == 0)
    def _():
        xb_ref[...] = x_ref[...].astype(BF16)

    o_ref[...] = jnp.dot(xb_ref[...], w_ref[...], preferred_element_type=F32).astype(o_ref.dtype)


def _proj(x2, w, out_dtype, tm=1024, tn=512):
    t, kdim = x2.shape
    n = w.shape[1]
    return pl.pallas_call(
        _proj_kernel,
        grid=(t // tm, n // tn),
        in_specs=[pl.BlockSpec((tm, kdim), lambda i, j: (i, 0)), pl.BlockSpec((kdim, tn), lambda i, j: (0, j))],
        out_specs=pl.BlockSpec((tm, tn), lambda i, j: (i, j)),
        out_shape=jax.ShapeDtypeStruct((t, n), out_dtype),
        scratch_shapes=[pltpu.VMEM((tm, kdim), BF16)],
        compiler_params=_params(("parallel", "arbitrary"), 48),
        name="ssd_in_proj",
    )(x2, w)


def _dt_kernel(x_ref, w_ref, bias_ref, o_ref):
    h = jnp.dot(x_ref[...].astype(BF16), w_ref[...], preferred_element_type=F32) + bias_ref[...]
    o_ref[...] = jnp.maximum(h, 0.0) + jnp.log1p(jnp.exp(-jnp.abs(h)))


def _dt_proj(x2, w_dt, bias_row, tm=1024):
    t, kdim = x2.shape
    n = w_dt.shape[1]
    return pl.pallas_call(
        _dt_kernel,
        grid=(t // tm,),
        in_specs=[pl.BlockSpec((tm, kdim), lambda i: (i, 0)), _resident((kdim, n)), _resident((1, n))],
        out_specs=pl.BlockSpec((tm, n), lambda i: (i, 0)),
        out_shape=jax.ShapeDtypeStruct((t, n), F32),
        compiler_params=_params(("parallel",), 32),
        name="ssd_dt_proj",
    )(x2, w_dt, bias_row)


CONV_HALO = 16
CONV_LANES = 512
CONV_ROWS = 32


def _conv_kernel(prev_ref, cur_ref, next_ref, w_ref, b_ref, xs_ref, bm_ref, cm_ref, ext_ref):
    i = pl.program_id(1)
    tc = cur_ref.shape[0]
    has_prev = (i > 0).astype(F32)
    has_next = (i < pl.num_programs(1) - 1).astype(F32)
    ext_ref[0:CONV_HALO, :] = prev_ref[...].astype(F32) * has_prev
    ext_ref[CONV_HALO:CONV_HALO + tc, :] = cur_ref[...].astype(F32)
    ext_ref[CONV_HALO + tc:, :] = next_ref[...].astype(F32) * has_next
    for cb in range(CONV_DIM // CONV_LANES):
        lanes = slice(cb * CONV_LANES, (cb + 1) * CONV_LANES)
        taps = [w_ref[k:k + 1, lanes] for k in range(D_CONV)]
        bias = b_ref[:, lanes]
        for rb in range(tc // CONV_ROWS):
            base = CONV_HALO - CONV_PAD + rb * CONV_ROWS
            acc = bias + ext_ref[base:base + CONV_ROWS, lanes] * taps[0]
            for k in range(1, D_CONV):
                acc = acc + ext_ref[base + k:base + k + CONV_ROWS, lanes] * taps[k]
            out = _silu(acc).astype(BF16)
            rows = slice(rb * CONV_ROWS, (rb + 1) * CONV_ROWS)
            if cb < D_INNER // CONV_LANES:
                xs_ref[rows, lanes] = out
            elif cb < (D_INNER + BC_DIM) // CONV_LANES:
                off = cb * CONV_LANES - D_INNER
                bm_ref[rows, off:off + CONV_LANES] = out
            else:
                off = cb * CONV_LANES - D_INNER - BC_DIM
                cm_ref[rows, off:off + CONV_LANES] = out


def _conv(xbc, conv_w, conv_b, batch, seq, tc=256):
    t = xbc.shape[0]
    nt = seq // tc
    per = tc // CONV_HALO
    last_halo = t // CONV_HALO - 1
    row = lambda w: pl.BlockSpec((tc, w), lambda b, i: (b * nt + i, 0))
    return pl.pallas_call(
        _conv_kernel,
        grid=(batch, nt),
        in_specs=[pl.BlockSpec((CONV_HALO, CONV_DIM), lambda b, i: (jnp.maximum((b * nt + i) * per - 1, 0), 0)),
                  row(CONV_DIM),
                  pl.BlockSpec((CONV_HALO, CONV_DIM), lambda b, i: (jnp.minimum((b * nt + i + 1) * per, last_halo), 0)),
                  _resident((D_CONV, CONV_DIM)), _resident((1, CONV_DIM))],
        out_specs=[row(D_INNER), row(BC_DIM), row(BC_DIM)],
        out_shape=[jax.ShapeDtypeStruct((t, D_INNER), BF16), jax.ShapeDtypeStruct((t, BC_DIM), BF16),
                   jax.ShapeDtypeStruct((t, BC_DIM), BF16)],
        scratch_shapes=[pltpu.VMEM((tc + 2 * CONV_HALO, CONV_DIM), F32)],
        compiler_params=_params(("parallel", "parallel"), 48),
        name="ssd_conv",
    )(xbc, xbc, xbc, conv_w, conv_b)


def _split3(v):
    hi = v.astype(BF16)
    r = v - hi.astype(F32)
    mid = r.astype(BF16)
    lo = (r - mid.astype(F32)).astype(BF16)
    return hi, mid, lo


def _expand_heads(v, e):
    hi = v.astype(BF16)
    lo = (v - hi.astype(F32)).astype(BF16)
    return jnp.dot(hi, e, preferred_element_type=F32) + jnp.dot(lo, e, preferred_element_type=F32)


def _scan_kernel(xs_ref, bm_ref, cm_ref, dt_ref, alog_ref, e_ref, y_ref, state_ref):
    d = pl.program_id(1)
    c = pl.program_id(2)

    @pl.when(c == 0)
    def _():
        state_ref[...] = jnp.zeros_like(state_ref)

    row = lax.broadcasted_iota(jnp.int32, (CHUNK, CHUNK), 0)
    col = lax.broadcasted_iota(jnp.int32, (CHUNK, CHUNK), 1)
    mask = jnp.where(d == 0, row - col, col - row) >= 0
    cum = mask.astype(BF16)

    dt = dt_ref[0]
    a = dt * -jnp.exp(alog_ref[0])
    a_hi, a_mid, a_lo = _split3(a)
    acs = (jnp.dot(cum, a_hi, preferred_element_type=F32) + jnp.dot(cum, a_mid, preferred_element_type=F32)
           + jnp.dot(cum, a_lo, preferred_element_type=F32))
    tot = jnp.sum(a, axis=0, keepdims=True)
    acs_t = acs.T
    dt_t = dt.T

    e = e_ref[...]
    decay_in = _expand_heads(jnp.exp(acs), e)
    decay_out = _expand_heads(jnp.exp(tot - acs) * dt, e)
    decay_tot = _expand_heads(jnp.broadcast_to(jnp.exp(tot), (8, SSD_HEADS)), e)[0:1]

    for g in range(N_GROUPS):
        gl = slice(g * GROUP_WIDTH, (g + 1) * GROUP_WIDTH)
        bg = bm_ref[:, g * D_STATE:(g + 1) * D_STATE]
        cg = cm_ref[:, g * D_STATE:(g + 1) * D_STATE]
        cb = lax.dot_general(cg, bg, (((1,), (1,)), ((), ())), preferred_element_type=F32)
        state_g = state_ref[:, gl]
        y_state = jnp.dot(cg, state_g.astype(BF16), preferred_element_type=F32) * decay_in[:, gl]
        ys = []
        for hh in range(HEADS_PER_GROUP):
            h = g * HEADS_PER_GROUP + hh
            seg = acs[:, h:h + 1] - acs_t[h:h + 1, :]
            lmat = jnp.exp(jnp.where(mask, seg, -jnp.inf))
            m = (cb * lmat * dt_t[h:h + 1, :]).astype(BF16)
            xh = xs_ref[:, h * SSD_HEAD_DIM:(h + 1) * SSD_HEAD_DIM]
            ys.append(jnp.dot(m, xh, preferred_element_type=F32))
        y_ref[0, :, gl] = (jnp.concatenate(ys, axis=1) + y_state).astype(y_ref.dtype)
        xw = (xs_ref[:, gl].astype(F32) * decay_out[:, gl]).astype(BF16)
        bg_t = bg.astype(F32).T.astype(BF16)
        state_ref[:, gl] = state_g * decay_tot[:, gl] + jnp.dot(bg_t, xw, preferred_element_type=F32)


def _scan(xs, bm, cm, dt2, alog, batch, seq):
    t = xs.shape[0]
    nc = seq // CHUNK
    heads = jnp.arange(D_INNER, dtype=jnp.int32) // SSD_HEAD_DIM
    expand = (heads[None, :] == jnp.arange(SSD_HEADS, dtype=jnp.int32)[:, None]).astype(BF16)

    def chunk(b, d, c):
        return b * nc + jnp.where(d == 0, c, nc - 1 - c)

    row = lambda w: pl.BlockSpec((CHUNK, w), lambda b, d, c: (chunk(b, d, c), 0))
    return pl.pallas_call(
        _scan_kernel,
        grid=(batch, 2, nc),
        in_specs=[row(D_INNER), row(BC_DIM), row(BC_DIM),
                  pl.BlockSpec((1, CHUNK, SSD_HEADS), lambda b, d, c: (d, chunk(b, d, c), 0)),
                  pl.BlockSpec((1, 1, SSD_HEADS), lambda b, d, c: (d, 0, 0)),
                  pl.BlockSpec((SSD_HEADS, D_INNER), lambda b, d, c: (0, 0))],
        out_specs=pl.BlockSpec((1, CHUNK, D_INNER), lambda b, d, c: (d, chunk(b, d, c), 0)),
        out_shape=jax.ShapeDtypeStruct((2, t, D_INNER), BF16),
        scratch_shapes=[pltpu.VMEM((D_STATE, D_INNER), F32)],
        compiler_params=_params(("parallel", "arbitrary", "arbitrary"), 48),
        name="ssd_scan",
    )(xs, bm, cm, dt2, alog, expand)


def _ssd_out_kernel(yf_ref, yb_ref, xs_ref, z_ref, x_ref, p_ref, dskip_ref, nw_ref, wout_ref,
                    lng_ref, lnb_ref, wpe_ref, wpg_ref, out_ref):
    y = yf_ref[0].astype(F32) + yb_ref[0].astype(F32) + xs_ref[...].astype(F32) * dskip_ref[...]
    y = _rmsnorm(y * _silu(z_ref[...].astype(F32)), nw_ref[...])
    h = jnp.dot(y.astype(BF16), wout_ref[...], preferred_element_type=F32)
    out_ref[...] = _deepnorm_ple(x_ref[...], h, p_ref[...], lng_ref[...], lnb_ref[...], wpe_ref[...], wpg_ref[...])


def _ssd_out(y2, xs, z, x2, p1, dskip, nw, wout, lng, lnb, wpe, wpg, tm=256):
    t = x2.shape[0]
    row = lambda w: pl.BlockSpec((tm, w), lambda i: (i, 0))
    ydir = lambda d: pl.BlockSpec((1, tm, D_INNER), lambda i: (d, i, 0))
    return pl.pallas_call(
        _ssd_out_kernel,
        grid=(t // tm,),
        in_specs=[ydir(0), ydir(1), row(D_INNER), row(D_INNER), row(D_MODEL), row(PLE_DIM),
                  _resident((1, D_INNER)), _resident((1, D_INNER)), _resident((D_INNER, D_MODEL)),
                  _resident((1, D_MODEL)), _resident((1, D_MODEL)),
                  _resident((PLE_DIM, D_MODEL)), _resident((D_MODEL, D_MODEL))],
        out_specs=row(D_MODEL),
        out_shape=jax.ShapeDtypeStruct((t, D_MODEL), F32),
        compiler_params=_params(("parallel",), 56),
        name="ssd_out",
    )(y2, y2, xs, z, x2, p1, dskip, nw, wout, lng, lnb, wpe, wpg)


def kernel(x, p, positions, ln_g, ln_b, ple_w_proj, ple_w_gate, mla_w_in, mla_q_norm, mla_kv_norm, mla_w_uq,
           mla_w_ukv, mla_w_o, ssd_w_in, ssd_conv_w, ssd_conv_b, ssd_dt_bias, ssd_a_log, ssd_d, ssd_norm, ssd_w_out):
    batch, seq, _ = x.shape
    t = batch * seq
    x2 = x.reshape(t, D_MODEL)
    p2 = p.reshape(DEPTH, t, PLE_DIM)
    row = lambda v: v.reshape(1, -1)

    inv_freq = 1.0 / (ROPE_BASE ** (jnp.arange(0, QK_ROPE, 2, dtype=F32) / QK_ROPE))
    freq_row = jnp.concatenate([inv_freq, inv_freq, jnp.zeros((ROPE_PAD - QK_ROPE,), F32)]).reshape(1, ROPE_PAD)
    ct, sa, sb = _rope_tables(positions.astype(F32).reshape(t, 1), freq_row)

    lat = Q_LORA + KV_LORA
    w_in = mla_w_in[0]
    w_in_p = jnp.concatenate([w_in[:, :lat + QK_ROPE], jnp.zeros((D_MODEL, ROPE_PAD - QK_ROPE), F32),
                              w_in[:, lat + QK_ROPE:]], axis=1).astype(BF16)
    wuq_p = jnp.pad(mla_w_uq[0].reshape(Q_LORA, MLA_HEADS, QK_HEAD).transpose(1, 0, 2),
                    ((0, 0), (0, 0), (0, QK_PAD - QK_HEAD))).astype(BF16)
    wukv_p = mla_w_ukv[0].reshape(KV_LORA, MLA_HEADS, QK_NOPE + V_HEAD).transpose(1, 0, 2).astype(BF16)

    cqn, ckvn, kr, gate = _mla_in(x2, w_in_p, row(mla_q_norm[0]), row(mla_kv_norm[0]), ct, sa, sb)
    qt, k, vt = _mla_up(cqn, ckvn, kr, ct, sa, sb, wuq_p, wukv_p, batch, seq)
    o = _attention(qt, k, vt).reshape(t, MLA_WIDTH)
    x2 = _mla_out(o, gate, x2, p2[0], mla_w_o[0].astype(BF16), row(ln_g[0]), row(ln_b[0]),
                  ple_w_proj[0].astype(BF16), ple_w_gate[0].astype(BF16))

    w_ssd = ssd_w_in[0].astype(BF16)
    z = _proj(x2, w_ssd[:, :D_INNER], BF16)
    xbc = _proj(x2, w_ssd[:, D_INNER:D_INNER + CONV_DIM], BF16)
    dt = _dt_proj(x2, w_ssd[:, D_INNER + CONV_DIM:], ssd_dt_bias[0].reshape(1, 2 * SSD_HEADS))
    xs, bm, cm = _conv(xbc, ssd_conv_w[0], row(ssd_conv_b[0]), batch, seq)
    dt2 = dt.reshape(t, 2, SSD_HEADS).transpose(1, 0, 2)
    y2 = _scan(xs, bm, cm, dt2, ssd_a_log[0].reshape(2, 1, SSD_HEADS), batch, seq)
    dskip = jnp.repeat(ssd_d[0], SSD_HEAD_DIM).reshape(1, D_INNER)
    x2 = _ssd_out(y2, xs, z, x2, p2[1], dskip, row(ssd_norm[0]), ssd_w_out[0].astype(BF16),
                  row(ln_g[1]), row(ln_b[1]), ple_w_proj[1].astype(BF16), ple_w_gate[1].astype(BF16))
    return x2.reshape(batch, seq, D_MODEL)
```

```python
import functools

import jax
import jax.numpy as jnp
from jax import lax
from jax.experimental import pallas as pl
from jax.experimental.pallas import tpu as pltpu

F32 = jnp.float32
BF16 = jnp.bfloat16

D_MODEL = 2048
DEPTH = 2
ALPHA = (2 * DEPTH) ** 0.25
PLE_DIM = 256

MLA_HEADS = 16
Q_LORA = 512
KV_LORA = 512
QK_NOPE = 128
QK_ROPE = 64
V_HEAD = 128
QK_HEAD = QK_NOPE + QK_ROPE
MLA_WIDTH = MLA_HEADS * V_HEAD
ROPE_BASE = 10000.0
ATTN_SCALE = QK_HEAD ** -0.5
LOG2_E = 1.4426950408889634
QK_PAD = 256
ROPE_PAD = 128

D_INNER = 4096
SSD_HEAD_DIM = 64
SSD_HEADS = 64
N_GROUPS = 8
HEADS_PER_GROUP = 8
D_STATE = 128
D_CONV = 7
CONV_PAD = D_CONV // 2
GROUP_WIDTH = HEADS_PER_GROUP * SSD_HEAD_DIM
BC_DIM = N_GROUPS * D_STATE
CONV_DIM = D_INNER + 2 * BC_DIM
CHUNK = 128
LANES = 128

LN_EPS = 1e-5
RMS_EPS = 1e-6

MIB = 1024 * 1024


def _params(semantics, vmem_mib):
    return pltpu.CompilerParams(dimension_semantics=semantics, vmem_limit_bytes=vmem_mib * MIB)


def _resident(shape, index=None):
    index = (0,) * len(shape) if index is None else index
    return pl.BlockSpec(shape, lambda *_: index, pipeline_mode=pl.Buffered(1))


def _rmsnorm(v, g):
    return v * lax.rsqrt(jnp.mean(v * v, axis=-1, keepdims=True) + RMS_EPS) * g


def _layernorm(v, g, b):
    mu = jnp.mean(v, axis=-1, keepdims=True)
    d = v - mu
    var = jnp.mean(d * d, axis=-1, keepdims=True)
    return d * lax.rsqrt(var + LN_EPS) * g + b


def _silu(v):
    return v * jax.nn.sigmoid(v)


def _rope(u, ct, sa, sb):
    return u * ct + pltpu.roll(u, QK_ROPE // 2, 1) * sa + pltpu.roll(u, ROPE_PAD - QK_ROPE // 2, 1) * sb


def _rope_table_kernel(pos_ref, freq_ref, ct_ref, sa_ref, sb_ref):
    ang = pos_ref[...] * freq_ref[...]
    c = jnp.cos(ang)
    s = jnp.sin(ang)
    lane = lax.broadcasted_iota(jnp.int32, ang.shape, 1)
    half = QK_ROPE // 2
    ct_ref[...] = jnp.where(lane < QK_ROPE, c, 0.0)
    sa_ref[...] = jnp.where((lane >= half) & (lane < QK_ROPE), s, 0.0)
    sb_ref[...] = jnp.where(lane < half, -s, 0.0)


def _rope_tables(pos_f32, freq_row, tm=2048):
    t = pos_f32.shape[0]
    tab = jax.ShapeDtypeStruct((t, ROPE_PAD), F32)
    row = pl.BlockSpec((tm, ROPE_PAD), lambda i: (i, 0))
    return pl.pallas_call(
        _rope_table_kernel,
        grid=(t // tm,),
        in_specs=[pl.BlockSpec((tm, 1), lambda i: (i, 0)), pl.BlockSpec((1, ROPE_PAD), lambda i: (0, 0))],
        out_specs=[row, row, row],
        out_shape=[tab, tab, tab],
        compiler_params=_params(("parallel",), 32),
        name="rope_tables",
    )(pos_f32, freq_row)


def _mla_in_kernel(x_ref, w_ref, qn_ref, kvn_ref, ct_ref, sa_ref, sb_ref, cq_ref, ckv_ref, kr_ref, g_ref):
    xb = x_ref[...].astype(BF16)
    lat = Q_LORA + KV_LORA + ROPE_PAD
    h = jnp.dot(xb, w_ref[:, :lat], preferred_element_type=F32)
    cq_ref[...] = _rmsnorm(h[:, :Q_LORA], qn_ref[...]).astype(BF16)
    ckv_ref[...] = _rmsnorm(h[:, Q_LORA:Q_LORA + KV_LORA], kvn_ref[...]).astype(BF16)
    kr_ref[...] = _rope(h[:, Q_LORA + KV_LORA:], ct_ref[...], sa_ref[...], sb_ref[...]).astype(BF16)
    z = jnp.dot(xb, w_ref[:, lat:], preferred_element_type=F32)
    g_ref[...] = _silu(z).astype(BF16)


def _mla_in(x2, w_in_p, qn, kvn, ct, sa, sb, tm=512):
    t = x2.shape[0]
    n = w_in_p.shape[1]
    row = lambda w: pl.BlockSpec((tm, w), lambda i: (i, 0))
    return pl.pallas_call(
        _mla_in_kernel,
        grid=(t // tm,),
        in_specs=[row(D_MODEL), _resident((D_MODEL, n)), _resident((1, Q_LORA)), _resident((1, KV_LORA)),
                  row(ROPE_PAD), row(ROPE_PAD), row(ROPE_PAD)],
        out_specs=[row(Q_LORA), row(KV_LORA), row(ROPE_PAD), row(MLA_WIDTH)],
        out_shape=[jax.ShapeDtypeStruct((t, Q_LORA), BF16), jax.ShapeDtypeStruct((t, KV_LORA), BF16),
                   jax.ShapeDtypeStruct((t, ROPE_PAD), BF16), jax.ShapeDtypeStruct((t, MLA_WIDTH), BF16)],
        compiler_params=_params(("parallel",), 48),
        name="mla_in",
    )(x2, w_in_p, qn, kvn, ct, sa, sb)


def _mla_up_kernel(cq_ref, ckv_ref, kr_ref, ct_ref, sa_ref, sb_ref, wuq_ref, wukv_ref, qt_ref, k_ref, vt_ref):
    q = jnp.dot(cq_ref[...], wuq_ref[0], preferred_element_type=F32) * (ATTN_SCALE * LOG2_E)
    qt_ref[0, 0, :QK_NOPE, :] = q[:, :QK_NOPE].T.astype(BF16)
    qt_ref[0, 0, QK_NOPE:, :] = _rope(q[:, QK_NOPE:], ct_ref[...], sa_ref[...], sb_ref[...]).T.astype(BF16)
    kv = jnp.dot(ckv_ref[...], wukv_ref[0], preferred_element_type=F32)
    k_ref[0, 0, :, :QK_NOPE] = kv[:, :QK_NOPE].astype(BF16)
    k_ref[0, 0, :, QK_NOPE:] = kr_ref[...]
    vt_ref[0, 0] = kv[:, QK_NOPE:].T.astype(BF16)


def _mla_up(cqn, ckvn, kr, ct, sa, sb, wuq_p, wukv_p, batch, seq, tm=1024):
    nt = seq // tm
    row = lambda w: pl.BlockSpec((tm, w), lambda b, i, h: (b * nt + i, 0))
    wspec = pl.BlockSpec((1, Q_LORA, QK_PAD), lambda b, i, h: (h, 0, 0))
    feat = lambda w: pl.BlockSpec((1, 1, w, tm), lambda b, i, h: (b, h, 0, i))
    return pl.pallas_call(
        _mla_up_kernel,
        grid=(batch, nt, MLA_HEADS),
        in_specs=[row(Q_LORA), row(KV_LORA), row(ROPE_PAD), row(ROPE_PAD), row(ROPE_PAD), row(ROPE_PAD),
                  wspec, wspec],
        out_specs=[feat(QK_PAD), pl.BlockSpec((1, 1, tm, QK_PAD), lambda b, i, h: (b, h, i, 0)), feat(V_HEAD)],
        out_shape=[jax.ShapeDtypeStruct((batch, MLA_HEADS, QK_PAD, seq), BF16),
                   jax.ShapeDtypeStruct((batch, MLA_HEADS, seq, QK_PAD), BF16),
                   jax.ShapeDtypeStruct((batch, MLA_HEADS, V_HEAD, seq), BF16)],
        compiler_params=_params(("parallel", "parallel", "arbitrary"), 32),
        name="mla_up",
    )(cqn, ckvn, kr, ct, sa, sb, wuq_p, wukv_p)


def _attn_kernel(qt_ref, k_ref, vt_ref, o_ref, s_ref, *, tk, n_col):
    tq = qt_ref.shape[3]
    w = tq // n_col
    n_chunks = k_ref.shape[2] // tk

    def scores(c, slot, g):
        s = jnp.dot(k_ref[0, 0, pl.ds(c * tk, tk), :], qt_ref[0, 0, :, g * w:(g + 1) * w],
                    preferred_element_type=F32)
        s_ref[slot, :, g * w:(g + 1) * w] = s
        return jnp.max(s, axis=0, keepdims=True)

    def consume(c, slot, g, m, m_chunk, l, acc):
        m_new = jnp.maximum(m, m_chunk)
        alpha = jnp.exp2(m - m_new)
        p = jnp.exp2(s_ref[slot, :, g * w:(g + 1) * w] - m_new)
        l = alpha * l + jnp.sum(p.reshape(tk // 8, 8, w), axis=0)
        pv = jnp.dot(vt_ref[0, 0, :, pl.ds(c * tk, tk)], p.astype(BF16), preferred_element_type=F32)
        return m_new, l, alpha * acc + pv

    carry = [(jnp.full((1, w), -jnp.inf, F32), scores(0, 0, g), jnp.zeros((8, w), F32),
              jnp.zeros((V_HEAD, w), F32)) for g in range(n_col)]
    for c in range(n_chunks):
        slot = c % 2
        for g in range(n_col):
            m, m_chunk, l, acc = carry[g]
            m_next = scores(c + 1, 1 - slot, g) if c + 1 < n_chunks else m_chunk
            m, l, acc = consume(c, slot, g, m, m_chunk, l, acc)
            carry[g] = (m, m_next, l, acc)
    for g in range(n_col):
        _, _, l, acc = carry[g]
        o_ref[0, g * w:(g + 1) * w, :] = (acc / jnp.sum(l, axis=0, keepdims=True)).T.astype(o_ref.dtype)


def _attention(qt, k, vt, tq=1024, tk=512, n_col=2):
    batch, heads, seq, _ = k.shape
    assert seq % (2 * tk) == 0 and seq % tq == 0
    return pl.pallas_call(
        functools.partial(_attn_kernel, tk=tk, n_col=n_col),
        grid=(batch, heads, seq // tq),
        in_specs=[pl.BlockSpec((1, 1, QK_PAD, tq), lambda b, h, i: (b, h, 0, i)),
                  pl.BlockSpec((1, 1, seq, QK_PAD), lambda b, h, i: (b, h, 0, 0)),
                  pl.BlockSpec((1, 1, V_HEAD, seq), lambda b, h, i: (b, h, 0, 0))],
        out_specs=pl.BlockSpec((1, tq, V_HEAD), lambda b, h, i: (b, i, h)),
        out_shape=jax.ShapeDtypeStruct((batch, seq, heads * V_HEAD), BF16),
        scratch_shapes=[pltpu.VMEM((2, tk, tq), F32)],
        compiler_params=_params(("parallel", "parallel", "arbitrary"), 48),
        name="mla_attention",
    )(qt, k, vt)


def _deepnorm_ple(x, h, p, lng, lnb, wpe, wpg):
    y = _layernorm(ALPHA * x + h, lng, lnb)
    emb = jnp.dot(p.astype(BF16), wpe, preferred_element_type=F32)
    gate = jax.nn.sigmoid(jnp.dot(y.astype(BF16), wpg, preferred_element_type=F32))
    return y + emb * gate


def _mla_out_kernel(o_ref, g_ref, x_ref, p_ref, wo_ref, lng_ref, lnb_ref, wpe_ref, wpg_ref, out_ref):
    og = (o_ref[...].astype(F32) * g_ref[...].astype(F32)).astype(BF16)
    h = jnp.dot(og, wo_ref[...], preferred_element_type=F32)
    out_ref[...] = _deepnorm_ple(x_ref[...], h, p_ref[...], lng_ref[...], lnb_ref[...], wpe_ref[...], wpg_ref[...])


def _mla_out(o, g, x2, p0, wo, lng, lnb, wpe, wpg, tm=512):
    t = x2.shape[0]
    row = lambda w: pl.BlockSpec((tm, w), lambda i: (i, 0))
    return pl.pallas_call(
        _mla_out_kernel,
        grid=(t // tm,),
        in_specs=[row(MLA_WIDTH), row(MLA_WIDTH), row(D_MODEL), row(PLE_DIM),
                  _resident((MLA_WIDTH, D_MODEL)), _resident((1, D_MODEL)), _resident((1, D_MODEL)),
                  _resident((PLE_DIM, D_MODEL)), _resident((D_MODEL, D_MODEL))],
        out_specs=row(D_MODEL),
        out_shape=jax.ShapeDtypeStruct((t, D_MODEL), F32),
        compiler_params=_params(("parallel",), 56),
        name="mla_out",
    )(o, g, x2, p0, wo, lng, lnb, wpe, wpg)


def _proj_kernel(x_ref, w_ref, o_ref, xb_ref):
    @pl.when(pl.program_id(1) == 0)
    def _():
        xb_ref[...] = x_ref[...].astype(BF16)

    o_ref[...] = jnp.dot(xb_ref[...], w_ref[...], preferred_element_type=F32).astype(o_ref.dtype)


def _proj(x2, w, n, out_dtype, tm=1024, tn=512):
    t, kdim = x2.shape
    return pl.pallas_call(
        _proj_kernel,
        grid=(t // tm, n // tn),
        in_specs=[pl.BlockSpec((tm, kdim), lambda i, j: (i, 0)), pl.BlockSpec((kdim, tn), lambda i, j: (0, j))],
        out_specs=pl.BlockSpec((tm, tn), lambda i, j: (i, j)),
        out_shape=jax.ShapeDtypeStruct((t, n), out_dtype),
        scratch_shapes=[pltpu.VMEM((tm, kdim), BF16)],
        compiler_params=_params(("parallel", "arbitrary"), 48),
        name="ssd_in_proj",
    )(x2, w)


def _dt_kernel(x_ref, w_ref, bias_ref, o_ref):
    h = jnp.dot(x_ref[...].astype(BF16), w_ref[...], preferred_element_type=F32) + bias_ref[...]
    o_ref[...] = jnp.maximum(h, 0.0) + jnp.log1p(jnp.exp(-jnp.abs(h)))


def _dt_proj(x2, w, col_block, bias_row, tm=1024):
    t, kdim = x2.shape
    n = bias_row.shape[1]
    return pl.pallas_call(
        _dt_kernel,
        grid=(t // tm,),
        in_specs=[pl.BlockSpec((tm, kdim), lambda i: (i, 0)), _resident((kdim, n), (0, col_block)), _resident((1, n))],
        out_specs=pl.BlockSpec((tm, n), lambda i: (i, 0)),
        out_shape=jax.ShapeDtypeStruct((t, n), F32),
        compiler_params=_params(("parallel",), 32),
        name="ssd_dt_proj",
    )(x2, w, bias_row)


CONV_HALO = 16
CONV_COLS = 2048
CONV_ROWS = 64


def _conv_kernel(prev_ref, cur_ref, next_ref, w_ref, b_ref, out_ref, ext_ref):
    i = pl.program_id(1)
    tc = cur_ref.shape[0]
    has_prev = (i > 0).astype(F32)
    has_next = (i < pl.num_programs(1) - 1).astype(F32)
    for ct in range(CONV_COLS // LANES):
        lanes = slice(ct * LANES, (ct + 1) * LANES)
        ext_ref[ct, 0:CONV_HALO, :] = prev_ref[:, lanes].astype(F32) * has_prev
        ext_ref[ct, CONV_HALO:CONV_HALO + tc, :] = cur_ref[:, lanes].astype(F32)
        ext_ref[ct, CONV_HALO + tc:, :] = next_ref[:, lanes].astype(F32) * has_next
    for ct in range(CONV_COLS // LANES):
        lanes = slice(ct * LANES, (ct + 1) * LANES)
        taps = [w_ref[k:k + 1, lanes] for k in range(D_CONV)]
        bias = b_ref[:, lanes]
        for rb in range(tc // CONV_ROWS):
            base = CONV_HALO - CONV_PAD + rb * CONV_ROWS
            acc = bias + ext_ref[ct, base:base + CONV_ROWS, :] * taps[0]
            for k in range(1, D_CONV):
                acc = acc + ext_ref[ct, base + k:base + k + CONV_ROWS, :] * taps[k]
            out_ref[rb * CONV_ROWS:(rb + 1) * CONV_ROWS, lanes] = _silu(acc).astype(BF16)


def _conv(zx, conv_w, conv_b, batch, seq, tc=512):
    t = zx.shape[0]
    nt = seq // tc
    per = tc // CONV_HALO
    last_halo = t // CONV_HALO - 1
    skip = D_INNER // CONV_COLS
    return pl.pallas_call(
        _conv_kernel,
        grid=(batch, nt, CONV_DIM // CONV_COLS),
        in_specs=[pl.BlockSpec((CONV_HALO, CONV_COLS),
                               lambda b, i, j: (jnp.maximum((b * nt + i) * per - 1, 0), j + skip)),
                  pl.BlockSpec((tc, CONV_COLS), lambda b, i, j: (b * nt + i, j + skip)),
                  pl.BlockSpec((CONV_HALO, CONV_COLS),
                               lambda b, i, j: (jnp.minimum((b * nt + i + 1) * per, last_halo), j + skip)),
                  pl.BlockSpec((D_CONV, CONV_COLS), lambda b, i, j: (0, j)),
                  pl.BlockSpec((1, CONV_COLS), lambda b, i, j: (0, j))],
        out_specs=pl.BlockSpec((tc, CONV_COLS), lambda b, i, j: (b * nt + i, j)),
        out_shape=jax.ShapeDtypeStruct((t, CONV_DIM), BF16),
        scratch_shapes=[pltpu.VMEM((CONV_COLS // LANES, tc + 2 * CONV_HALO, LANES), F32)],
        compiler_params=_params(("parallel", "parallel", "parallel"), 32),
        name="ssd_conv",
    )(zx, zx, zx, conv_w, conv_b)


def _split3(v):
    hi = v.astype(BF16)
    r = v - hi.astype(F32)
    mid = r.astype(BF16)
    lo = (r - mid.astype(F32)).astype(BF16)
    return hi, mid, lo


def _scan_kernel(xs_ref, bm_ref, cm_ref, dt_ref, alog_ref, e_ref, y_ref, state_ref):
    d = pl.program_id(1)
    c = pl.program_id(2)

    @pl.when(c == 0)
    def _():
        state_ref[...] = jnp.zeros_like(state_ref)

    row = lax.broadcasted_iota(jnp.int32, (CHUNK, CHUNK), 0)
    col = lax.broadcasted_iota(jnp.int32, (CHUNK, CHUNK), 1)
    mask = jnp.where(d == 0, row - col, col - row) >= 0
    cum = mask.astype(BF16)
    low_half = col < SSD_HEAD_DIM

    dt = dt_ref[0]
    a = dt * -jnp.exp(alog_ref[0])
    a_hi, a_mid, a_lo = _split3(a)
    acs = (jnp.dot(cum, a_hi, preferred_element_type=F32) + jnp.dot(cum, a_mid, preferred_element_type=F32)
           + jnp.dot(cum, a_lo, preferred_element_type=F32))
    tot = jnp.sum(a, axis=0, keepdims=True)
    src_t = (acs - jnp.log(dt)).T

    e_in = jnp.exp(acs).astype(BF16)
    e_out = (jnp.exp(tot - acs) * dt).astype(BF16)
    e_tot = jnp.broadcast_to(jnp.exp(tot), (8, SSD_HEADS)).astype(BF16)
    zero = jnp.zeros((CHUNK, LANES), BF16)

    def prep(g):
        gl = slice(g * GROUP_WIDTH, (g + 1) * GROUP_WIDTH)
        bg = bm_ref[:, g * D_STATE:(g + 1) * D_STATE]
        cg = cm_ref[:, g * D_STATE:(g + 1) * D_STATE]
        cb = lax.dot_general(cg, bg, (((1,), (1,)), ((), ())), preferred_element_type=F32)
        y_raw = jnp.dot(cg, state_ref[:, gl].astype(BF16), preferred_element_type=F32)
        e = e_ref[:, gl]
        decay_in = jnp.dot(e_in, e, preferred_element_type=F32)
        decay_out = jnp.dot(e_out, e, preferred_element_type=F32)
        decay_tot = jnp.dot(e_tot, e, preferred_element_type=F32)[0:1]
        return bg, cb, y_raw * decay_in, decay_out, decay_tot

    ready = prep(0)
    for g in range(N_GROUPS):
        gl = slice(g * GROUP_WIDTH, (g + 1) * GROUP_WIDTH)
        bg, cb, y_state, decay_out, decay_tot = ready
        if g + 1 < N_GROUPS:
            ready = prep(g + 1)
        ys = []
        for pair in range(HEADS_PER_GROUP // 2):
            ms = []
            for h in (g * HEADS_PER_GROUP + 2 * pair, g * HEADS_PER_GROUP + 2 * pair + 1):
                seg = acs[:, h:h + 1] - src_t[h:h + 1, :]
                ms.append((cb * jnp.exp(jnp.where(mask, seg, -jnp.inf))).astype(BF16))
            xp = xs_ref[:, g * GROUP_WIDTH + pair * LANES:g * GROUP_WIDTH + (pair + 1) * LANES]
            x_diag = jnp.concatenate([jnp.where(low_half, xp, zero), jnp.where(low_half, zero, xp)], axis=0)
            ys.append(jnp.dot(jnp.concatenate(ms, axis=1), x_diag, preferred_element_type=F32))
        y_ref[0, :, gl] = (jnp.concatenate(ys, axis=1) + y_state).astype(y_ref.dtype)
        xw = (xs_ref[:, gl].astype(F32) * decay_out).astype(BF16)
        bg_t = bg.astype(F32).T.astype(BF16)
        state_ref[:, gl] = state_ref[:, gl] * decay_tot + jnp.dot(bg_t, xw, preferred_element_type=F32)


def _scan(xbc, dt2, alog, batch, seq):
    t = xbc.shape[0]
    nc = seq // CHUNK
    heads = jnp.arange(D_INNER, dtype=jnp.int32) // SSD_HEAD_DIM
    expand = (heads[None, :] == jnp.arange(SSD_HEADS, dtype=jnp.int32)[:, None]).astype(BF16)

    def chunk(b, d, c):
        return b * nc + jnp.where(d == 0, c, nc - 1 - c)

    b_block = D_INNER // BC_DIM
    return pl.pallas_call(
        _scan_kernel,
        grid=(batch, 2, nc),
        in_specs=[pl.BlockSpec((CHUNK, D_INNER), lambda b, d, c: (chunk(b, d, c), 0)),
                  pl.BlockSpec((CHUNK, BC_DIM), lambda b, d, c: (chunk(b, d, c), b_block)),
                  pl.BlockSpec((CHUNK, BC_DIM), lambda b, d, c: (chunk(b, d, c), b_block + 1)),
                  pl.BlockSpec((1, CHUNK, SSD_HEADS), lambda b, d, c: (d, chunk(b, d, c), 0)),
                  pl.BlockSpec((1, 1, SSD_HEADS), lambda b, d, c: (d, 0, 0)),
                  pl.BlockSpec((SSD_HEADS, D_INNER), lambda b, d, c: (0, 0))],
        out_specs=pl.BlockSpec((1, CHUNK, D_INNER), lambda b, d, c: (d, chunk(b, d, c), 0)),
        out_shape=jax.ShapeDtypeStruct((2, t, D_INNER), BF16),
        scratch_shapes=[pltpu.VMEM((D_STATE, D_INNER), F32)],
        compiler_params=_params(("parallel", "arbitrary", "arbitrary"), 48),
        name="ssd_scan",
    )(xbc, xbc, xbc, dt2, alog, expand)


def _ssd_out_kernel(yf_ref, yb_ref, xs_ref, z_ref, x_ref, p_ref, dskip_ref, nw_ref, wout_ref,
                    lng_ref, lnb_ref, wpe_ref, wpg_ref, out_ref):
    y = yf_ref[0].astype(F32) + yb_ref[0].astype(F32) + xs_ref[...].astype(F32) * dskip_ref[...]
    y = _rmsnorm(y * _silu(z_ref[...].astype(F32)), nw_ref[...])
    h = jnp.dot(y.astype(BF16), wout_ref[...], preferred_element_type=F32)
    out_ref[...] = _deepnorm_ple(x_ref[...], h, p_ref[...], lng_ref[...], lnb_ref[...], wpe_ref[...], wpg_ref[...])


def _ssd_out(y2, xbc, zx, x2, p1, dskip, nw, wout, lng, lnb, wpe, wpg, tm=256):
    t = x2.shape[0]
    row = lambda w: pl.BlockSpec((tm, w), lambda i: (i, 0))
    ydir = lambda d: pl.BlockSpec((1, tm, D_INNER), lambda i: (d, i, 0))
    return pl.pallas_call(
        _ssd_out_kernel,
        grid=(t // tm,),
        in_specs=[ydir(0), ydir(1), row(D_INNER), row(D_INNER), row(D_MODEL), row(PLE_DIM),
                  _resident((1, D_INNER)), _resident((1, D_INNER)), _resident((D_INNER, D_MODEL)),
                  _resident((1, D_MODEL)), _resident((1, D_MODEL)),
                  _resident((PLE_DIM, D_MODEL)), _resident((D_MODEL, D_MODEL))],
        out_specs=row(D_MODEL),
        out_shape=jax.ShapeDtypeStruct((t, D_MODEL), F32),
        compiler_params=_params(("parallel",), 56),
        name="ssd_out",
    )(y2, y2, xbc, zx, x2, p1, dskip, nw, wout, lng, lnb, wpe, wpg)


def kernel(x, p, positions, ln_g, ln_b, ple_w_proj, ple_w_gate, mla_w_in, mla_q_norm, mla_kv_norm, mla_w_uq,
           mla_w_ukv, mla_w_o, ssd_w_in, ssd_conv_w, ssd_conv_b, ssd_dt_bias, ssd_a_log, ssd_d, ssd_norm, ssd_w_out):
    batch, seq, _ = x.shape
    t = batch * seq
    x2 = x.reshape(t, D_MODEL)
    p2 = p.reshape(DEPTH, t, PLE_DIM)
    row = lambda v: v.reshape(1, -1)

    inv_freq = 1.0 / (ROPE_BASE ** (jnp.arange(0, QK_ROPE, 2, dtype=F32) / QK_ROPE))
    freq_row = jnp.concatenate([inv_freq, inv_freq, jnp.zeros((ROPE_PAD - QK_ROPE,), F32)]).reshape(1, ROPE_PAD)
    ct, sa, sb = _rope_tables(positions.astype(F32).reshape(t, 1), freq_row)

    lat = Q_LORA + KV_LORA
    w_in = mla_w_in[0]
    w_in_p = jnp.concatenate([w_in[:, :lat + QK_ROPE], jnp.zeros((D_MODEL, ROPE_PAD - QK_ROPE), F32),
                              w_in[:, lat + QK_ROPE:]], axis=1).astype(BF16)
    wuq_p = jnp.pad(mla_w_uq[0].reshape(Q_LORA, MLA_HEADS, QK_HEAD).transpose(1, 0, 2),
                    ((0, 0), (0, 0), (0, QK_PAD - QK_HEAD))).astype(BF16)
    wukv_p = mla_w_ukv[0].reshape(KV_LORA, MLA_HEADS, QK_NOPE + V_HEAD).transpose(1, 0, 2).astype(BF16)

    cqn, ckvn, kr, gate = _mla_in(x2, w_in_p, row(mla_q_norm[0]), row(mla_kv_norm[0]), ct, sa, sb)
    qt, k, vt = _mla_up(cqn, ckvn, kr, ct, sa, sb, wuq_p, wukv_p, batch, seq)
    o = _attention(qt, k, vt).reshape(t, MLA_WIDTH)
    x2 = _mla_out(o, gate, x2, p2[0], mla_w_o[0].astype(BF16), row(ln_g[0]), row(ln_b[0]),
                  ple_w_proj[0].astype(BF16), ple_w_gate[0].astype(BF16))

    w_ssd = ssd_w_in[0].astype(BF16)
    zx = _proj(x2, w_ssd, D_INNER + CONV_DIM, BF16)
    dt = _dt_proj(x2, w_ssd, (D_INNER + CONV_DIM) // (2 * SSD_HEADS), ssd_dt_bias[0].reshape(1, 2 * SSD_HEADS))
    xbc = _conv(zx, ssd_conv_w[0], row(ssd_conv_b[0]), batch, seq)
    dt2 = dt.reshape(t, 2, SSD_HEADS).transpose(1, 0, 2)
    y2 = _scan(xbc, dt2, ssd_a_log[0].reshape(2, 1, SSD_HEADS), batch, seq)
    dskip = jnp.repeat(ssd_d[0], SSD_HEAD_DIM).reshape(1, D_INNER)
    x2 = _ssd_out(y2, xbc, zx, x2, p2[1], dskip, row(ssd_norm[0]), ssd_w_out[0].astype(BF16),
                  row(ln_g[1]), row(ln_b[1]), ple_w_proj[1].astype(BF16), ple_w_gate[1].astype(BF16))
    return x2.reshape(batch, seq, D_MODEL)
```

```python
import functools

import jax
import jax.numpy as jnp
from jax import lax
from jax.experimental import pallas as pl
from jax.experimental.pallas import tpu as pltpu

F32 = jnp.float32
BF16 = jnp.bfloat16

D_MODEL = 2048
DEPTH = 2
ALPHA = (2 * DEPTH) ** 0.25
PLE_DIM = 256

MLA_HEADS = 16
Q_LORA = 512
KV_LORA = 512
QK_NOPE = 128
QK_ROPE = 64
V_HEAD = 128
QK_HEAD = QK_NOPE + QK_ROPE
MLA_WIDTH = MLA_HEADS * V_HEAD
ROPE_BASE = 10000.0
ATTN_SCALE = QK_HEAD ** -0.5
LOG2_E = 1.4426950408889634
QK_PAD = 256
ROPE_PAD = 128

D_INNER = 4096
SSD_HEAD_DIM = 64
SSD_HEADS = 64
N_GROUPS = 8
HEADS_PER_GROUP = 8
D_STATE = 128
D_CONV = 7
CONV_PAD = D_CONV // 2
GROUP_WIDTH = HEADS_PER_GROUP * SSD_HEAD_DIM
BC_DIM = N_GROUPS * D_STATE
CONV_DIM = D_INNER + 2 * BC_DIM
CHUNK = 128
LANES = 128

LN_EPS = 1e-5
RMS_EPS = 1e-6

MIB = 1024 * 1024


def _params(semantics, vmem_mib):
    return pltpu.CompilerParams(dimension_semantics=semantics, vmem_limit_bytes=vmem_mib * MIB)


def _resident(shape, index=None):
    index = (0,) * len(shape) if index is None else index
    return pl.BlockSpec(shape, lambda *_: index, pipeline_mode=pl.Buffered(1))


def _rmsnorm(v, g):
    return v * lax.rsqrt(jnp.mean(v * v, axis=-1, keepdims=True) + RMS_EPS) * g


def _layernorm(v, g, b):
    mu = jnp.mean(v, axis=-1, keepdims=True)
    d = v - mu
    var = jnp.mean(d * d, axis=-1, keepdims=True)
    return d * lax.rsqrt(var + LN_EPS) * g + b


def _silu(v):
    return v * jax.nn.sigmoid(v)


def _rope(u, ct, sa, sb):
    return u * ct + pltpu.roll(u, QK_ROPE // 2, 1) * sa + pltpu.roll(u, ROPE_PAD - QK_ROPE // 2, 1) * sb


def _rope_table_kernel(pos_ref, freq_ref, ct_ref, sa_ref, sb_ref, cos_t_ref, sin_t_ref):
    ang = pos_ref[...] * freq_ref[...]
    c = jnp.cos(ang)
    s = jnp.sin(ang)
    lane = lax.broadcasted_iota(jnp.int32, ang.shape, 1)
    half = QK_ROPE // 2
    ct_ref[...] = jnp.where(lane < QK_ROPE, c, 0.0)
    sa_ref[...] = jnp.where((lane >= half) & (lane < QK_ROPE), s, 0.0)
    sb_ref[...] = jnp.where(lane < half, -s, 0.0)
    cos_t_ref[...] = c.T[:half]
    sin_t_ref[...] = s.T[:half]


def _rope_tables(pos_f32, freq_row, tm=2048):
    t = pos_f32.shape[0]
    tab = jax.ShapeDtypeStruct((t, ROPE_PAD), F32)
    tab_t = jax.ShapeDtypeStruct((QK_ROPE // 2, t), F32)
    row = pl.BlockSpec((tm, ROPE_PAD), lambda i: (i, 0))
    col = pl.BlockSpec((QK_ROPE // 2, tm), lambda i: (0, i))
    return pl.pallas_call(
        _rope_table_kernel,
        grid=(t // tm,),
        in_specs=[pl.BlockSpec((tm, 1), lambda i: (i, 0)), pl.BlockSpec((1, ROPE_PAD), lambda i: (0, 0))],
        out_specs=[row, row, row, col, col],
        out_shape=[tab, tab, tab, tab_t, tab_t],
        compiler_params=_params(("parallel",), 32),
        name="rope_tables",
    )(pos_f32, freq_row)


def _mla_in_kernel(x_ref, w_ref, qn_ref, kvn_ref, ct_ref, sa_ref, sb_ref, cqt_ref, ckv_ref, kr_ref, g_ref):
    xb = x_ref[...].astype(BF16)
    lat = Q_LORA + KV_LORA + ROPE_PAD
    h = jnp.dot(xb, w_ref[:, :lat], preferred_element_type=F32)
    cqt_ref[...] = _rmsnorm(h[:, :Q_LORA], qn_ref[...]).T.astype(BF16)
    ckv_ref[...] = _rmsnorm(h[:, Q_LORA:Q_LORA + KV_LORA], kvn_ref[...]).astype(BF16)
    kr_ref[...] = _rope(h[:, Q_LORA + KV_LORA:], ct_ref[...], sa_ref[...], sb_ref[...]).astype(BF16)
    z = jnp.dot(xb, w_ref[:, lat:], preferred_element_type=F32)
    g_ref[...] = _silu(z).astype(BF16)


def _mla_in(x2, w_in_p, qn, kvn, ct, sa, sb, tm=512):
    t = x2.shape[0]
    n = w_in_p.shape[1]
    row = lambda w: pl.BlockSpec((tm, w), lambda i: (i, 0))
    return pl.pallas_call(
        _mla_in_kernel,
        grid=(t // tm,),
        in_specs=[row(D_MODEL), _resident((D_MODEL, n)), _resident((1, Q_LORA)), _resident((1, KV_LORA)),
                  row(ROPE_PAD), row(ROPE_PAD), row(ROPE_PAD)],
        out_specs=[pl.BlockSpec((Q_LORA, tm), lambda i: (0, i)), row(KV_LORA), row(ROPE_PAD), row(MLA_WIDTH)],
        out_shape=[jax.ShapeDtypeStruct((Q_LORA, t), BF16), jax.ShapeDtypeStruct((t, KV_LORA), BF16),
                   jax.ShapeDtypeStruct((t, ROPE_PAD), BF16), jax.ShapeDtypeStruct((t, MLA_WIDTH), BF16)],
        compiler_params=_params(("parallel",), 48),
        name="mla_in",
    )(x2, w_in_p, qn, kvn, ct, sa, sb)


def _mla_up_kernel(cqt_ref, ckv_ref, kr_ref, cos_ref, sin_ref, wuqt_ref, wukv_ref, qt_ref, k_ref, vt_ref):
    qt = jnp.dot(wuqt_ref[0], cqt_ref[...], preferred_element_type=F32) * (ATTN_SCALE * LOG2_E)
    half = QK_ROPE // 2
    t1 = qt[QK_NOPE:QK_NOPE + half]
    t2 = qt[QK_NOPE + half:QK_HEAD]
    c = cos_ref[...]
    s = sin_ref[...]
    qt_ref[0, 0, :QK_NOPE, :] = qt[:QK_NOPE].astype(BF16)
    qt_ref[0, 0, QK_NOPE:QK_NOPE + half, :] = (t1 * c - t2 * s).astype(BF16)
    qt_ref[0, 0, QK_NOPE + half:QK_HEAD, :] = (t2 * c + t1 * s).astype(BF16)
    qt_ref[0, 0, QK_HEAD:, :] = jnp.zeros((QK_PAD - QK_HEAD, qt.shape[1]), BF16)
    kv = jnp.dot(ckv_ref[...], wukv_ref[0], preferred_element_type=F32)
    k_ref[0, 0, :, :QK_NOPE] = kv[:, :QK_NOPE].astype(BF16)
    k_ref[0, 0, :, QK_NOPE:] = kr_ref[...]
    vt_ref[0, 0] = kv[:, QK_NOPE:].T.astype(BF16)


def _mla_up(cqt, ckvn, kr, cos_t, sin_t, wuqt_p, wukv_p, batch, seq, tm=1024):
    nt = seq // tm
    row = lambda w: pl.BlockSpec((tm, w), lambda b, i, h: (b * nt + i, 0))
    col = lambda w: pl.BlockSpec((w, tm), lambda b, i, h: (0, b * nt + i))
    feat = lambda w: pl.BlockSpec((1, 1, w, tm), lambda b, i, h: (b, h, 0, i))
    return pl.pallas_call(
        _mla_up_kernel,
        grid=(batch, nt, MLA_HEADS),
        in_specs=[col(Q_LORA), row(KV_LORA), row(ROPE_PAD), col(QK_ROPE // 2), col(QK_ROPE // 2),
                  pl.BlockSpec((1, QK_PAD, Q_LORA), lambda b, i, h: (h, 0, 0)),
                  pl.BlockSpec((1, KV_LORA, QK_PAD), lambda b, i, h: (h, 0, 0))],
        out_specs=[feat(QK_PAD), pl.BlockSpec((1, 1, tm, QK_PAD), lambda b, i, h: (b, h, i, 0)), feat(V_HEAD)],
        out_shape=[jax.ShapeDtypeStruct((batch, MLA_HEADS, QK_PAD, seq), BF16),
                   jax.ShapeDtypeStruct((batch, MLA_HEADS, seq, QK_PAD), BF16),
                   jax.ShapeDtypeStruct((batch, MLA_HEADS, V_HEAD, seq), BF16)],
        compiler_params=_params(("parallel", "parallel", "arbitrary"), 32),
        name="mla_up",
    )(cqt, ckvn, kr, cos_t, sin_t, wuqt_p, wukv_p)


def _attn_kernel(qt_ref, k_ref, vt_ref, o_ref, s_ref, *, tk, n_col):
    tq = qt_ref.shape[3]
    w = tq // n_col
    n_chunks = k_ref.shape[2] // tk

    def scores(c, slot, g):
        s = jnp.dot(k_ref[0, 0, pl.ds(c * tk, tk), :], qt_ref[0, 0, :, g * w:(g + 1) * w],
                    preferred_element_type=F32)
        s_ref[slot, :, g * w:(g + 1) * w] = s
        return jnp.max(s, axis=0, keepdims=True)

    def consume(c, slot, g, m, m_chunk, l, acc):
        m_new = jnp.maximum(m, m_chunk)
        alpha = jnp.exp2(m - m_new)
        p = jnp.exp2(s_ref[slot, :, g * w:(g + 1) * w] - m_new)
        l = alpha * l + jnp.sum(p.reshape(tk // 8, 8, w), axis=0)
        pv = jnp.dot(vt_ref[0, 0, :, pl.ds(c * tk, tk)], p.astype(BF16), preferred_element_type=F32)
        return m_new, l, alpha * acc + pv

    carry = [(jnp.full((1, w), -jnp.inf, F32), scores(0, 0, g), jnp.zeros((8, w), F32),
              jnp.zeros((V_HEAD, w), F32)) for g in range(n_col)]
    for c in range(n_chunks):
        slot = c % 2
        for g in range(n_col):
            m, m_chunk, l, acc = carry[g]
            m_next = scores(c + 1, 1 - slot, g) if c + 1 < n_chunks else m_chunk
            m, l, acc = consume(c, slot, g, m, m_chunk, l, acc)
            carry[g] = (m, m_next, l, acc)
    for g in range(n_col):
        _, _, l, acc = carry[g]
        o_ref[0, g * w:(g + 1) * w, :] = (acc / jnp.sum(l, axis=0, keepdims=True)).T.astype(o_ref.dtype)


def _attention(qt, k, vt, tq=1024, tk=512, n_col=2):
    batch, heads, seq, _ = k.shape
    assert seq % (2 * tk) == 0 and seq % tq == 0
    return pl.pallas_call(
        functools.partial(_attn_kernel, tk=tk, n_col=n_col),
        grid=(batch, heads, seq // tq),
        in_specs=[pl.BlockSpec((1, 1, QK_PAD, tq), lambda b, h, i: (b, h, 0, i)),
                  pl.BlockSpec((1, 1, seq, QK_PAD), lambda b, h, i: (b, h, 0, 0)),
                  pl.BlockSpec((1, 1, V_HEAD, seq), lambda b, h, i: (b, h, 0, 0))],
        out_specs=pl.BlockSpec((1, tq, V_HEAD), lambda b, h, i: (b, i, h)),
        out_shape=jax.ShapeDtypeStruct((batch, seq, heads * V_HEAD), BF16),
        scratch_shapes=[pltpu.VMEM((2, tk, tq), F32)],
        compiler_params=_params(("parallel", "parallel", "arbitrary"), 48),
        name="mla_attention",
    )(qt, k, vt)


def _deepnorm_ple(x, h, p, lng, lnb, wpe, wpg):
    y = _layernorm(ALPHA * x + h, lng, lnb)
    emb = jnp.dot(p.astype(BF16), wpe, preferred_element_type=F32)
    gate = jax.nn.sigmoid(jnp.dot(y.astype(BF16), wpg, preferred_element_type=F32))
    return y + emb * gate


def _mla_out_kernel(o_ref, g_ref, x_ref, p_ref, wo_ref, lng_ref, lnb_ref, wpe_ref, wpg_ref, out_ref):
    og = (o_ref[...].astype(F32) * g_ref[...].astype(F32)).astype(BF16)
    h = jnp.dot(og, wo_ref[...], preferred_element_type=F32)
    out_ref[...] = _deepnorm_ple(x_ref[...], h, p_ref[...], lng_ref[...], lnb_ref[...], wpe_ref[...], wpg_ref[...])


def _mla_out(o, g, x2, p0, wo, lng, lnb, wpe, wpg, tm=512):
    t = x2.shape[0]
    row = lambda w: pl.BlockSpec((tm, w), lambda i: (i, 0))
    return pl.pallas_call(
        _mla_out_kernel,
        grid=(t // tm,),
        in_specs=[row(MLA_WIDTH), row(MLA_WIDTH), row(D_MODEL), row(PLE_DIM),
                  _resident((MLA_WIDTH, D_MODEL)), _resident((1, D_MODEL)), _resident((1, D_MODEL)),
                  _resident((PLE_DIM, D_MODEL)), _resident((D_MODEL, D_MODEL))],
        out_specs=row(D_MODEL),
        out_shape=jax.ShapeDtypeStruct((t, D_MODEL), F32),
        compiler_params=_params(("parallel",), 56),
        name="mla_out",
    )(o, g, x2, p0, wo, lng, lnb, wpe, wpg)


def _proj_kernel(x_ref, w_ref, o_ref, xb_ref):
    @pl.when(pl.program_id(1) == 0)
    def _():
        xb_ref[...] = x_ref[...].astype(BF16)

    o_ref[...] = jnp.dot(xb_ref[...], w_ref[...], preferred_element_type=F32).astype(o_ref.dtype)


def _proj(x2, w, n, out_dtype, tm=1024, tn=1024):
    t, kdim = x2.shape
    return pl.pallas_call(
        _proj_kernel,
        grid=(t // tm, n // tn),
        in_specs=[pl.BlockSpec((tm, kdim), lambda i, j: (i, 0)), pl.BlockSpec((kdim, tn), lambda i, j: (0, j))],
        out_specs=pl.BlockSpec((tm, tn), lambda i, j: (i, j)),
        out_shape=jax.ShapeDtypeStruct((t, n), out_dtype),
        scratch_shapes=[pltpu.VMEM((tm, kdim), BF16)],
        compiler_params=_params(("parallel", "arbitrary"), 48),
        name="ssd_in_proj",
    )(x2, w)


def _dt_kernel(x_ref, w_ref, bias_ref, o_ref):
    h = jnp.dot(x_ref[...].astype(BF16), w_ref[...], preferred_element_type=F32) + bias_ref[...]
    o_ref[...] = jnp.maximum(h, 0.0) + jnp.log1p(jnp.exp(-jnp.abs(h)))


def _dt_proj(x2, w, col_block, bias_row, tm=1024):
    t, kdim = x2.shape
    n = bias_row.shape[1]
    return pl.pallas_call(
        _dt_kernel,
        grid=(t // tm,),
        in_specs=[pl.BlockSpec((tm, kdim), lambda i: (i, 0)), _resident((kdim, n), (0, col_block)), _resident((1, n))],
        out_specs=pl.BlockSpec((tm, n), lambda i: (i, 0)),
        out_shape=jax.ShapeDtypeStruct((t, n), F32),
        compiler_params=_params(("parallel",), 32),
        name="ssd_dt_proj",
    )(x2, w, bias_row)


CONV_HALO = 16
CONV_COLS = 2048
CONV_ROWS = 64


def _conv_kernel(prev_ref, cur_ref, next_ref, w_ref, b_ref, out_ref, ext_ref):
    i = pl.program_id(1)
    tc = cur_ref.shape[0]
    has_prev = (i > 0).astype(F32)
    has_next = (i < pl.num_programs(1) - 1).astype(F32)
    for ct in range(CONV_COLS // LANES):
        lanes = slice(ct * LANES, (ct + 1) * LANES)
        ext_ref[ct, 0:CONV_HALO, :] = prev_ref[:, lanes].astype(F32) * has_prev
        ext_ref[ct, CONV_HALO:CONV_HALO + tc, :] = cur_ref[:, lanes].astype(F32)
        ext_ref[ct, CONV_HALO + tc:, :] = next_ref[:, lanes].astype(F32) * has_next
    for ct in range(CONV_COLS // LANES):
        lanes = slice(ct * LANES, (ct + 1) * LANES)
        taps = [w_ref[k:k + 1, lanes] for k in range(D_CONV)]
        bias = b_ref[:, lanes]
        for rb in range(tc // CONV_ROWS):
            base = CONV_HALO - CONV_PAD + rb * CONV_ROWS
            acc = bias + ext_ref[ct, base:base + CONV_ROWS, :] * taps[0]
            for k in range(1, D_CONV):
                acc = acc + ext_ref[ct, base + k:base + k + CONV_ROWS, :] * taps[k]
            out_ref[rb * CONV_ROWS:(rb + 1) * CONV_ROWS, lanes] = _silu(acc).astype(BF16)


def _conv(zx, conv_w, conv_b, batch, seq, tc=512):
    t = zx.shape[0]
    nt = seq // tc
    per = tc // CONV_HALO
    last_halo = t // CONV_HALO - 1
    skip = D_INNER // CONV_COLS
    return pl.pallas_call(
        _conv_kernel,
        grid=(batch, nt, CONV_DIM // CONV_COLS),
        in_specs=[pl.BlockSpec((CONV_HALO, CONV_COLS),
                               lambda b, i, j: (jnp.maximum((b * nt + i) * per - 1, 0), j + skip)),
                  pl.BlockSpec((tc, CONV_COLS), lambda b, i, j: (b * nt + i, j + skip)),
                  pl.BlockSpec((CONV_HALO, CONV_COLS),
                               lambda b, i, j: (jnp.minimum((b * nt + i + 1) * per, last_halo), j + skip)),
                  pl.BlockSpec((D_CONV, CONV_COLS), lambda b, i, j: (0, j)),
                  pl.BlockSpec((1, CONV_COLS), lambda b, i, j: (0, j))],
        out_specs=pl.BlockSpec((tc, CONV_COLS), lambda b, i, j: (b * nt + i, j)),
        out_shape=jax.ShapeDtypeStruct((t, CONV_DIM), BF16),
        scratch_shapes=[pltpu.VMEM((CONV_COLS // LANES, tc + 2 * CONV_HALO, LANES), F32)],
        compiler_params=_params(("parallel", "parallel", "parallel"), 32),
        name="ssd_conv",
    )(zx, zx, zx, conv_w, conv_b)


def _split3(v):
    hi = v.astype(BF16)
    r = v - hi.astype(F32)
    mid = r.astype(BF16)
    lo = (r - mid.astype(F32)).astype(BF16)
    return hi, mid, lo


def _scan_kernel(xs_ref, bm_ref, cm_ref, dt_ref, alog_ref, e_ref, y_ref, state_ref):
    d = pl.program_id(1)
    c = pl.program_id(2)

    @pl.when(c == 0)
    def _():
        state_ref[...] = jnp.zeros_like(state_ref)

    row = lax.broadcasted_iota(jnp.int32, (CHUNK, CHUNK), 0)
    col = lax.broadcasted_iota(jnp.int32, (CHUNK, CHUNK), 1)
    mask = jnp.where(d == 0, row - col, col - row) >= 0
    cum = mask.astype(BF16)
    low_half = col < SSD_HEAD_DIM

    dt = dt_ref[0]
    a = dt * -jnp.exp(alog_ref[0])
    a_hi, a_mid, a_lo = _split3(a)
    acs = (jnp.dot(cum, a_hi, preferred_element_type=F32) + jnp.dot(cum, a_mid, preferred_element_type=F32)
           + jnp.dot(cum, a_lo, preferred_element_type=F32))
    tot = jnp.sum(a, axis=0, keepdims=True)
    src_t = (acs - jnp.log(dt)).T

    e_in = jnp.exp(acs).astype(BF16)
    e_out = (jnp.exp(tot - acs) * dt).astype(BF16)
    e_tot = jnp.broadcast_to(jnp.exp(tot), (8, SSD_HEADS)).astype(BF16)
    zero = jnp.zeros((CHUNK, LANES), BF16)

    def prep(g):
        gl = slice(g * GROUP_WIDTH, (g + 1) * GROUP_WIDTH)
        bg = bm_ref[:, g * D_STATE:(g + 1) * D_STATE]
        cg = cm_ref[:, g * D_STATE:(g + 1) * D_STATE]
        cb = lax.dot_general(cg, bg, (((1,), (1,)), ((), ())), preferred_element_type=F32)
        y_raw = jnp.dot(cg, state_ref[:, gl].astype(BF16), preferred_element_type=F32)
        e = e_ref[:, gl]
        decay_in = jnp.dot(e_in, e, preferred_element_type=F32)
        decay_out = jnp.dot(e_out, e, preferred_element_type=F32)
        decay_tot = jnp.dot(e_tot, e, preferred_element_type=F32)[0:1]
        return bg, cb, y_raw * decay_in, decay_out, decay_tot

    ready = prep(0)
    for g in range(N_GROUPS):
        gl = slice(g * GROUP_WIDTH, (g + 1) * GROUP_WIDTH)
        bg, cb, y_state, decay_out, decay_tot = ready
        if g + 1 < N_GROUPS:
            ready = prep(g + 1)
        ys = []
        for pair in range(HEADS_PER_GROUP // 2):
            ms = []
            for h in (g * HEADS_PER_GROUP + 2 * pair, g * HEADS_PER_GROUP + 2 * pair + 1):
                seg = acs[:, h:h + 1] - src_t[h:h + 1, :]
                ms.append((cb * jnp.exp(jnp.where(mask, seg, -jnp.inf))).astype(BF16))
            xp = xs_ref[:, g * GROUP_WIDTH + pair * LANES:g * GROUP_WIDTH + (pair + 1) * LANES]
            x_diag = jnp.concatenate([jnp.where(low_half, xp, zero), jnp.where(low_half, zero, xp)], axis=0)
            ys.append(jnp.dot(jnp.concatenate(ms, axis=1), x_diag, preferred_element_type=F32))
        y_ref[0, :, gl] = (jnp.concatenate(ys, axis=1) + y_state).astype(y_ref.dtype)
        xw = (xs_ref[:, gl].astype(F32) * decay_out).astype(BF16)
        bg_t = bg.astype(F32).T.astype(BF16)
        state_ref[:, gl] = state_ref[:, gl] * decay_tot + jnp.dot(bg_t, xw, preferred_element_type=F32)


def _scan(xbc, dt2, alog, batch, seq):
    t = xbc.shape[0]
    nc = seq // CHUNK
    heads = jnp.arange(D_INNER, dtype=jnp.int32) // SSD_HEAD_DIM
    expand = (heads[None, :] == jnp.arange(SSD_HEADS, dtype=jnp.int32)[:, None]).astype(BF16)

    def chunk(b, d, c):
        return b * nc + jnp.where(d == 0, c, nc - 1 - c)

    b_block = D_INNER // BC_DIM
    return pl.pallas_call(
        _scan_kernel,
        grid=(batch, 2, nc),
        in_specs=[pl.BlockSpec((CHUNK, D_INNER), lambda b, d, c: (chunk(b, d, c), 0)),
                  pl.BlockSpec((CHUNK, BC_DIM), lambda b, d, c: (chunk(b, d, c), b_block)),
                  pl.BlockSpec((CHUNK, BC_DIM), lambda b, d, c: (chunk(b, d, c), b_block + 1)),
                  pl.BlockSpec((1, CHUNK, SSD_HEADS), lambda b, d, c: (d, chunk(b, d, c), 0)),
                  pl.BlockSpec((1, 1, SSD_HEADS), lambda b, d, c: (d, 0, 0)),
                  pl.BlockSpec((SSD_HEADS, D_INNER), lambda b, d, c: (0, 0))],
        out_specs=pl.BlockSpec((1, CHUNK, D_INNER), lambda b, d, c: (d, chunk(b, d, c), 0)),
        out_shape=jax.ShapeDtypeStruct((2, t, D_INNER), BF16),
        scratch_shapes=[pltpu.VMEM((D_STATE, D_INNER), F32)],
        compiler_params=_params(("parallel", "arbitrary", "arbitrary"), 48),
        name="ssd_scan",
    )(xbc, xbc, xbc, dt2, alog, expand)


def _ssd_out_kernel(yf_ref, yb_ref, xs_ref, z_ref, x_ref, p_ref, dskip_ref, nw_ref, wout_ref,
                    lng_ref, lnb_ref, wpe_ref, wpg_ref, out_ref):
    y = yf_ref[0].astype(F32) + yb_ref[0].astype(F32) + xs_ref[...].astype(F32) * dskip_ref[...]
    y = _rmsnorm(y * _silu(z_ref[...].astype(F32)), nw_ref[...])
    h = jnp.dot(y.astype(BF16), wout_ref[...], preferred_element_type=F32)
    out_ref[...] = _deepnorm_ple(x_ref[...], h, p_ref[...], lng_ref[...], lnb_ref[...], wpe_ref[...], wpg_ref[...])


def _ssd_out(y2, xbc, zx, x2, p1, dskip, nw, wout, lng, lnb, wpe, wpg, tm=256):
    t = x2.shape[0]
    row = lambda w: pl.BlockSpec((tm, w), lambda i: (i, 0))
    ydir = lambda d: pl.BlockSpec((1, tm, D_INNER), lambda i: (d, i, 0))
    return pl.pallas_call(
        _ssd_out_kernel,
        grid=(t // tm,),
        in_specs=[ydir(0), ydir(1), row(D_INNER), row(D_INNER), row(D_MODEL), row(PLE_DIM),
                  _resident((1, D_INNER)), _resident((1, D_INNER)), _resident((D_INNER, D_MODEL)),
                  _resident((1, D_MODEL)), _resident((1, D_MODEL)),
                  _resident((PLE_DIM, D_MODEL)), _resident((D_MODEL, D_MODEL))],
        out_specs=row(D_MODEL),
        out_shape=jax.ShapeDtypeStruct((t, D_MODEL), F32),
        compiler_params=_params(("parallel",), 56),
        name="ssd_out",
    )(y2, y2, xbc, zx, x2, p1, dskip, nw, wout, lng, lnb, wpe, wpg)


def kernel(x, p, positions, ln_g, ln_b, ple_w_proj, ple_w_gate, mla_w_in, mla_q_norm, mla_kv_norm, mla_w_uq,
           mla_w_ukv, mla_w_o, ssd_w_in, ssd_conv_w, ssd_conv_b, ssd_dt_bias, ssd_a_log, ssd_d, ssd_norm, ssd_w_out):
    batch, seq, _ = x.shape
    t = batch * seq
    x2 = x.reshape(t, D_MODEL)
    p2 = p.reshape(DEPTH, t, PLE_DIM)
    row = lambda v: v.reshape(1, -1)

    inv_freq = 1.0 / (ROPE_BASE ** (jnp.arange(0, QK_ROPE, 2, dtype=F32) / QK_ROPE))
    freq_row = jnp.concatenate([inv_freq, inv_freq, jnp.zeros((ROPE_PAD - QK_ROPE,), F32)]).reshape(1, ROPE_PAD)
    ct, sa, sb, cos_t, sin_t = _rope_tables(positions.astype(F32).reshape(t, 1), freq_row)

    lat = Q_LORA + KV_LORA
    w_in = mla_w_in[0]
    w_in_p = jnp.concatenate([w_in[:, :lat + QK_ROPE], jnp.zeros((D_MODEL, ROPE_PAD - QK_ROPE), F32),
                              w_in[:, lat + QK_ROPE:]], axis=1).astype(BF16)
    wuqt_p = jnp.pad(mla_w_uq[0].astype(BF16).reshape(Q_LORA, MLA_HEADS, QK_HEAD).transpose(1, 2, 0),
                     ((0, 0), (0, QK_PAD - QK_HEAD), (0, 0)))
    wukv_p = mla_w_ukv[0].reshape(KV_LORA, MLA_HEADS, QK_NOPE + V_HEAD).transpose(1, 0, 2).astype(BF16)

    cqt, ckvn, kr, gate = _mla_in(x2, w_in_p, row(mla_q_norm[0]), row(mla_kv_norm[0]), ct, sa, sb)
    qt, k, vt = _mla_up(cqt, ckvn, kr, cos_t, sin_t, wuqt_p, wukv_p, batch, seq)
    o = _attention(qt, k, vt).reshape(t, MLA_WIDTH)
    x2 = _mla_out(o, gate, x2, p2[0], mla_w_o[0].astype(BF16), row(ln_g[0]), row(ln_b[0]),
                  ple_w_proj[0].astype(BF16), ple_w_gate[0].astype(BF16))

    w_ssd = ssd_w_in[0].astype(BF16)
    zx = _proj(x2, w_ssd, D_INNER + CONV_DIM, BF16)
    dt = _dt_proj(x2, w_ssd, (D_INNER + CONV_DIM) // (2 * SSD_HEADS), ssd_dt_bias[0].reshape(1, 2 * SSD_HEADS))
    xbc = _conv(zx, ssd_conv_w[0], row(ssd_conv_b[0]), batch, seq)
    dt2 = dt.reshape(t, 2, SSD_HEADS).transpose(1, 0, 2)
    y2 = _scan(xbc, dt2, ssd_a_log[0].reshape(2, 1, SSD_HEADS), batch, seq)
    dskip = jnp.repeat(ssd_d[0], SSD_HEAD_DIM).reshape(1, D_INNER)
    x2 = _ssd_out(y2, xbc, zx, x2, p2[1], dskip, row(ssd_norm[0]), ssd_w_out[0].astype(BF16),
                  row(ln_g[1]), row(ln_b[1]), ple_w_proj[1].astype(BF16), ple_w_gate[1].astype(BF16))
    return x2.reshape(batch, seq, D_MODEL)
```

```python
import functools

import jax
import jax.numpy as jnp
from jax import lax
from jax.experimental import pallas as pl
from jax.experimental.pallas import tpu as pltpu

F32 = jnp.float32
BF16 = jnp.bfloat16

D_MODEL = 2048
DEPTH = 2
ALPHA = (2 * DEPTH) ** 0.25
PLE_DIM = 256

MLA_HEADS = 16
Q_LORA = 512
KV_LORA = 512
QK_NOPE = 128
QK_ROPE = 64
V_HEAD = 128
QK_HEAD = QK_NOPE + QK_ROPE
MLA_WIDTH = MLA_HEADS * V_HEAD
ROPE_BASE = 10000.0
ATTN_SCALE = QK_HEAD ** -0.5
LOG2_E = 1.4426950408889634
QK_PAD = 256
ROPE_PAD = 128

D_INNER = 4096
SSD_HEAD_DIM = 64
SSD_HEADS = 64
N_GROUPS = 8
HEADS_PER_GROUP = 8
D_STATE = 128
D_CONV = 7
CONV_PAD = D_CONV // 2
GROUP_WIDTH = HEADS_PER_GROUP * SSD_HEAD_DIM
BC_DIM = N_GROUPS * D_STATE
CONV_DIM = D_INNER + 2 * BC_DIM
CHUNK = 128
LANES = 128

LN_EPS = 1e-5
RMS_EPS = 1e-6

MIB = 1024 * 1024


def _params(semantics, vmem_mib):
    return pltpu.CompilerParams(dimension_semantics=semantics, vmem_limit_bytes=vmem_mib * MIB)


def _resident(shape, index=None):
    index = (0,) * len(shape) if index is None else index
    return pl.BlockSpec(shape, lambda *_: index, pipeline_mode=pl.Buffered(1))


def _rmsnorm(v, g):
    return v * lax.rsqrt(jnp.mean(v * v, axis=-1, keepdims=True) + RMS_EPS) * g


def _layernorm(v, g, b):
    mu = jnp.mean(v, axis=-1, keepdims=True)
    d = v - mu
    var = jnp.mean(d * d, axis=-1, keepdims=True)
    return d * lax.rsqrt(var + LN_EPS) * g + b


def _silu(v):
    return v * jax.nn.sigmoid(v)


def _rope(u, ct, sa, sb):
    return u * ct + pltpu.roll(u, QK_ROPE // 2, 1) * sa + pltpu.roll(u, ROPE_PAD - QK_ROPE // 2, 1) * sb


def _rope_table_kernel(pos_ref, freq_ref, ct_ref, sa_ref, sb_ref, cos_t_ref, sin_t_ref):
    ang = pos_ref[...] * freq_ref[...]
    c = jnp.cos(ang)
    s = jnp.sin(ang)
    lane = lax.broadcasted_iota(jnp.int32, ang.shape, 1)
    half = QK_ROPE // 2
    ct_ref[...] = jnp.where(lane < QK_ROPE, c, 0.0)
    sa_ref[...] = jnp.where((lane >= half) & (lane < QK_ROPE), s, 0.0)
    sb_ref[...] = jnp.where(lane < half, -s, 0.0)
    cos_t_ref[...] = c.T[:half]
    sin_t_ref[...] = s.T[:half]


def _rope_tables(pos_f32, freq_row, tm=2048):
    t = pos_f32.shape[0]
    tab = jax.ShapeDtypeStruct((t, ROPE_PAD), F32)
    tab_t = jax.ShapeDtypeStruct((QK_ROPE // 2, t), F32)
    row = pl.BlockSpec((tm, ROPE_PAD), lambda i: (i, 0))
    col = pl.BlockSpec((QK_ROPE // 2, tm), lambda i: (0, i))
    return pl.pallas_call(
        _rope_table_kernel,
        grid=(t // tm,),
        in_specs=[pl.BlockSpec((tm, 1), lambda i: (i, 0)), pl.BlockSpec((1, ROPE_PAD), lambda i: (0, 0))],
        out_specs=[row, row, row, col, col],
        out_shape=[tab, tab, tab, tab_t, tab_t],
        compiler_params=_params(("parallel",), 32),
        name="rope_tables",
    )(pos_f32, freq_row)


def _mla_in_kernel(x_ref, w_ref, qn_ref, kvn_ref, ct_ref, sa_ref, sb_ref, cqt_ref, ckv_ref, kr_ref, g_ref):
    xb = x_ref[...].astype(BF16)
    lat = Q_LORA + KV_LORA + ROPE_PAD
    h = jnp.dot(xb, w_ref[:, :lat], preferred_element_type=F32)
    cqt_ref[...] = _rmsnorm(h[:, :Q_LORA], qn_ref[...]).T.astype(BF16)
    ckv_ref[...] = _rmsnorm(h[:, Q_LORA:Q_LORA + KV_LORA], kvn_ref[...]).astype(BF16)
    kr_ref[...] = _rope(h[:, Q_LORA + KV_LORA:], ct_ref[...], sa_ref[...], sb_ref[...]).astype(BF16)
    z = jnp.dot(xb, w_ref[:, lat:], preferred_element_type=F32)
    g_ref[...] = _silu(z).astype(BF16)


def _mla_in(x2, w_in_p, qn, kvn, ct, sa, sb, tm=512):
    t = x2.shape[0]
    n = w_in_p.shape[1]
    row = lambda w: pl.BlockSpec((tm, w), lambda i: (i, 0))
    return pl.pallas_call(
        _mla_in_kernel,
        grid=(t // tm,),
        in_specs=[row(D_MODEL), _resident((D_MODEL, n)), _resident((1, Q_LORA)), _resident((1, KV_LORA)),
                  row(ROPE_PAD), row(ROPE_PAD), row(ROPE_PAD)],
        out_specs=[pl.BlockSpec((Q_LORA, tm), lambda i: (0, i)), row(KV_LORA), row(ROPE_PAD), row(MLA_WIDTH)],
        out_shape=[jax.ShapeDtypeStruct((Q_LORA, t), BF16), jax.ShapeDtypeStruct((t, KV_LORA), BF16),
                   jax.ShapeDtypeStruct((t, ROPE_PAD), BF16), jax.ShapeDtypeStruct((t, MLA_WIDTH), BF16)],
        compiler_params=_params(("parallel",), 48),
        name="mla_in",
    )(x2, w_in_p, qn, kvn, ct, sa, sb)


def _mla_up_kernel(cqt_ref, ckv_ref, kr_ref, cos_ref, sin_ref, wuqt_ref, wukv_ref, qt_ref, k_ref, vt_ref):
    qt = jnp.dot(wuqt_ref[0], cqt_ref[...], preferred_element_type=F32) * (ATTN_SCALE * LOG2_E)
    half = QK_ROPE // 2
    t1 = qt[QK_NOPE:QK_NOPE + half]
    t2 = qt[QK_NOPE + half:QK_HEAD]
    c = cos_ref[...]
    s = sin_ref[...]
    qt_ref[0, 0, :QK_NOPE, :] = qt[:QK_NOPE].astype(BF16)
    qt_ref[0, 0, QK_NOPE:QK_NOPE + half, :] = (t1 * c - t2 * s).astype(BF16)
    qt_ref[0, 0, QK_NOPE + half:QK_HEAD, :] = (t2 * c + t1 * s).astype(BF16)
    qt_ref[0, 0, QK_HEAD:, :] = jnp.zeros((QK_PAD - QK_HEAD, qt.shape[1]), BF16)
    kv = jnp.dot(ckv_ref[...], wukv_ref[0], preferred_element_type=F32)
    k_ref[0, 0, :, :QK_NOPE] = kv[:, :QK_NOPE].astype(BF16)
    k_ref[0, 0, :, QK_NOPE:] = kr_ref[...]
    vt_ref[0, 0] = kv[:, QK_NOPE:].T.astype(BF16)


def _mla_up(cqt, ckvn, kr, cos_t, sin_t, wuqt_p, wukv_p, batch, seq, tm=1024):
    nt = seq // tm
    row = lambda w: pl.BlockSpec((tm, w), lambda b, i, h: (b * nt + i, 0))
    col = lambda w: pl.BlockSpec((w, tm), lambda b, i, h: (0, b * nt + i))
    feat = lambda w: pl.BlockSpec((1, 1, w, tm), lambda b, i, h: (b, h, 0, i))
    return pl.pallas_call(
        _mla_up_kernel,
        grid=(batch, nt, MLA_HEADS),
        in_specs=[col(Q_LORA), row(KV_LORA), row(ROPE_PAD), col(QK_ROPE // 2), col(QK_ROPE // 2),
                  pl.BlockSpec((1, QK_PAD, Q_LORA), lambda b, i, h: (h, 0, 0)),
                  pl.BlockSpec((1, KV_LORA, QK_PAD), lambda b, i, h: (h, 0, 0))],
        out_specs=[feat(QK_PAD), pl.BlockSpec((1, 1, tm, QK_PAD), lambda b, i, h: (b, h, i, 0)), feat(V_HEAD)],
        out_shape=[jax.ShapeDtypeStruct((batch, MLA_HEADS, QK_PAD, seq), BF16),
                   jax.ShapeDtypeStruct((batch, MLA_HEADS, seq, QK_PAD), BF16),
                   jax.ShapeDtypeStruct((batch, MLA_HEADS, V_HEAD, seq), BF16)],
        compiler_params=_params(("parallel", "parallel", "arbitrary"), 32),
        name="mla_up",
    )(cqt, ckvn, kr, cos_t, sin_t, wuqt_p, wukv_p)


def _attn_kernel(qt_ref, k_ref, vt_ref, o_ref, s_ref, *, tk, n_col):
    tq = qt_ref.shape[3]
    w = tq // n_col
    n_chunks = k_ref.shape[2] // tk

    def scores(c, slot, g):
        s = jnp.dot(k_ref[0, 0, pl.ds(c * tk, tk), :], qt_ref[0, 0, :, g * w:(g + 1) * w],
                    preferred_element_type=F32)
        s_ref[slot, :, g * w:(g + 1) * w] = s
        return jnp.max(s, axis=0, keepdims=True)

    def consume(c, slot, g, m, m_chunk, l, acc):
        m_new = jnp.maximum(m, m_chunk)
        alpha = jnp.exp2(m - m_new)
        p = jnp.exp2(s_ref[slot, :, g * w:(g + 1) * w] - m_new)
        l = alpha * l + jnp.sum(p.reshape(tk // 8, 8, w), axis=0)
        pv = jnp.dot(vt_ref[0, 0, :, pl.ds(c * tk, tk)], p.astype(BF16), preferred_element_type=F32)
        return m_new, l, alpha * acc + pv

    carry = [(jnp.full((1, w), -jnp.inf, F32), scores(0, 0, g), jnp.zeros((8, w), F32),
              jnp.zeros((V_HEAD, w), F32)) for g in range(n_col)]
    for c in range(n_chunks):
        slot = c % 2
        for g in range(n_col):
            m, m_chunk, l, acc = carry[g]
            m_next = scores(c + 1, 1 - slot, g) if c + 1 < n_chunks else m_chunk
            m, l, acc = consume(c, slot, g, m, m_chunk, l, acc)
            carry[g] = (m, m_next, l, acc)
    for g in range(n_col):
        _, _, l, acc = carry[g]
        o_ref[0, g * w:(g + 1) * w, :] = (acc / jnp.sum(l, axis=0, keepdims=True)).T.astype(o_ref.dtype)


def _attention(qt, k, vt, tq=1024, tk=512, n_col=2):
    batch, heads, seq, _ = k.shape
    assert seq % (2 * tk) == 0 and seq % tq == 0
    return pl.pallas_call(
        functools.partial(_attn_kernel, tk=tk, n_col=n_col),
        grid=(batch, heads, seq // tq),
        in_specs=[pl.BlockSpec((1, 1, QK_PAD, tq), lambda b, h, i: (b, h, 0, i)),
                  pl.BlockSpec((1, 1, seq, QK_PAD), lambda b, h, i: (b, h, 0, 0)),
                  pl.BlockSpec((1, 1, V_HEAD, seq), lambda b, h, i: (b, h, 0, 0))],
        out_specs=pl.BlockSpec((1, tq, V_HEAD), lambda b, h, i: (b, i, h)),
        out_shape=jax.ShapeDtypeStruct((batch, seq, heads * V_HEAD), BF16),
        scratch_shapes=[pltpu.VMEM((2, tk, tq), F32)],
        compiler_params=_params(("parallel", "parallel", "arbitrary"), 48),
        name="mla_attention",
    )(qt, k, vt)


def _deepnorm_ple(x, h, p, lng, lnb, wpe, wpg):
    y = _layernorm(ALPHA * x + h, lng, lnb)
    emb = jnp.dot(p.astype(BF16), wpe, preferred_element_type=F32)
    gate = jax.nn.sigmoid(jnp.dot(y.astype(BF16), wpg, preferred_element_type=F32))
    return y + emb * gate


def _mla_out_kernel(o_ref, g_ref, x_ref, p_ref, wo_ref, lng_ref, lnb_ref, wpe_ref, wpg_ref, out_ref):
    og = (o_ref[...].astype(F32) * g_ref[...].astype(F32)).astype(BF16)
    h = jnp.dot(og, wo_ref[...], preferred_element_type=F32)
    out_ref[...] = _deepnorm_ple(x_ref[...], h, p_ref[...], lng_ref[...], lnb_ref[...], wpe_ref[...], wpg_ref[...])


def _mla_out(o, g, x2, p0, wo, lng, lnb, wpe, wpg, tm=512):
    t = x2.shape[0]
    row = lambda w: pl.BlockSpec((tm, w), lambda i: (i, 0))
    return pl.pallas_call(
        _mla_out_kernel,
        grid=(t // tm,),
        in_specs=[row(MLA_WIDTH), row(MLA_WIDTH), row(D_MODEL), row(PLE_DIM),
                  _resident((MLA_WIDTH, D_MODEL)), _resident((1, D_MODEL)), _resident((1, D_MODEL)),
                  _resident((PLE_DIM, D_MODEL)), _resident((D_MODEL, D_MODEL))],
        out_specs=row(D_MODEL),
        out_shape=jax.ShapeDtypeStruct((t, D_MODEL), F32),
        compiler_params=_params(("parallel",), 56),
        name="mla_out",
    )(o, g, x2, p0, wo, lng, lnb, wpe, wpg)


def _proj_kernel(x_ref, w_ref, o_ref, xb_ref):
    @pl.when(pl.program_id(1) == 0)
    def _():
        xb_ref[...] = x_ref[...].astype(BF16)

    o_ref[...] = jnp.dot(xb_ref[...], w_ref[...], preferred_element_type=F32).astype(o_ref.dtype)


def _proj(x2, w, n, out_dtype, tm=1024, tn=1024):
    t, kdim = x2.shape
    return pl.pallas_call(
        _proj_kernel,
        grid=(t // tm, n // tn),
        in_specs=[pl.BlockSpec((tm, kdim), lambda i, j: (i, 0)), pl.BlockSpec((kdim, tn), lambda i, j: (0, j))],
        out_specs=pl.BlockSpec((tm, tn), lambda i, j: (i, j)),
        out_shape=jax.ShapeDtypeStruct((t, n), out_dtype),
        scratch_shapes=[pltpu.VMEM((tm, kdim), BF16)],
        compiler_params=_params(("parallel", "arbitrary"), 48),
        name="ssd_in_proj",
    )(x2, w)


def _dt_kernel(x_ref, w_ref, bias_ref, o_ref):
    h = jnp.dot(x_ref[...].astype(BF16), w_ref[...], preferred_element_type=F32) + bias_ref[...]
    o_ref[...] = jnp.maximum(h, 0.0) + jnp.log1p(jnp.exp(-jnp.abs(h)))


def _dt_proj(x2, w, col_block, bias_row, tm=1024):
    t, kdim = x2.shape
    n = bias_row.shape[1]
    return pl.pallas_call(
        _dt_kernel,
        grid=(t // tm,),
        in_specs=[pl.BlockSpec((tm, kdim), lambda i: (i, 0)), _resident((kdim, n), (0, col_block)), _resident((1, n))],
        out_specs=pl.BlockSpec((tm, n), lambda i: (i, 0)),
        out_shape=jax.ShapeDtypeStruct((t, n), F32),
        compiler_params=_params(("parallel",), 32),
        name="ssd_dt_proj",
    )(x2, w, bias_row)


CONV_HALO = 16
CONV_COLS = 2048
CONV_ROWS = 64


def _conv_kernel(prev_ref, cur_ref, next_ref, w_ref, b_ref, out_ref, ext_ref):
    i = pl.program_id(1)
    tc = cur_ref.shape[0]
    has_prev = (i > 0).astype(F32)
    has_next = (i < pl.num_programs(1) - 1).astype(F32)
    for ct in range(CONV_COLS // LANES):
        lanes = slice(ct * LANES, (ct + 1) * LANES)
        ext_ref[ct, 0:CONV_HALO, :] = prev_ref[:, lanes].astype(F32) * has_prev
        ext_ref[ct, CONV_HALO:CONV_HALO + tc, :] = cur_ref[:, lanes].astype(F32)
        ext_ref[ct, CONV_HALO + tc:, :] = next_ref[:, lanes].astype(F32) * has_next
    for ct in range(CONV_COLS // LANES):
        lanes = slice(ct * LANES, (ct + 1) * LANES)
        taps = [w_ref[k:k + 1, lanes] for k in range(D_CONV)]
        bias = b_ref[:, lanes]
        for rb in range(tc // CONV_ROWS):
            base = CONV_HALO - CONV_PAD + rb * CONV_ROWS
            acc = bias + ext_ref[ct, base:base + CONV_ROWS, :] * taps[0]
            for k in range(1, D_CONV):
                acc = acc + ext_ref[ct, base + k:base + k + CONV_ROWS, :] * taps[k]
            out_ref[rb * CONV_ROWS:(rb + 1) * CONV_ROWS, lanes] = _silu(acc).astype(BF16)


def _conv(zx, conv_w, conv_b, batch, seq, tc=512):
    t = zx.shape[0]
    nt = seq // tc
    per = tc // CONV_HALO
    last_halo = t // CONV_HALO - 1
    skip = D_INNER // CONV_COLS
    return pl.pallas_call(
        _conv_kernel,
        grid=(batch, nt, CONV_DIM // CONV_COLS),
        in_specs=[pl.BlockSpec((CONV_HALO, CONV_COLS),
                               lambda b, i, j: (jnp.maximum((b * nt + i) * per - 1, 0), j + skip)),
                  pl.BlockSpec((tc, CONV_COLS), lambda b, i, j: (b * nt + i, j + skip)),
                  pl.BlockSpec((CONV_HALO, CONV_COLS),
                               lambda b, i, j: (jnp.minimum((b * nt + i + 1) * per, last_halo), j + skip)),
                  pl.BlockSpec((D_CONV, CONV_COLS), lambda b, i, j: (0, j)),
                  pl.BlockSpec((1, CONV_COLS), lambda b, i, j: (0, j))],
        out_specs=pl.BlockSpec((tc, CONV_COLS), lambda b, i, j: (b * nt + i, j)),
        out_shape=jax.ShapeDtypeStruct((t, CONV_DIM), BF16),
        scratch_shapes=[pltpu.VMEM((CONV_COLS // LANES, tc + 2 * CONV_HALO, LANES), F32)],
        compiler_params=_params(("parallel", "parallel", "parallel"), 32),
        name="ssd_conv",
    )(zx, zx, zx, conv_w, conv_b)


def _split3(v):
    hi = v.astype(BF16)
    r = v - hi.astype(F32)
    mid = r.astype(BF16)
    lo = (r - mid.astype(F32)).astype(BF16)
    return hi, mid, lo


def _scan_direction(xs_ref, bm_ref, cm_ref, dt_ref, alog, e_ref, y_ref, state_ref, reverse):
    row = lax.broadcasted_iota(jnp.int32, (CHUNK, CHUNK), 0)
    col = lax.broadcasted_iota(jnp.int32, (CHUNK, CHUNK), 1)
    mask = (col >= row) if reverse else (row >= col)
    cum = mask.astype(BF16)
    low_half = col < SSD_HEAD_DIM

    dt = dt_ref[0]
    a = dt * -jnp.exp(alog)
    a_hi, a_mid, a_lo = _split3(a)
    acs = (jnp.dot(cum, a_hi, preferred_element_type=F32) + jnp.dot(cum, a_mid, preferred_element_type=F32)
           + jnp.dot(cum, a_lo, preferred_element_type=F32))
    tot = jnp.sum(a, axis=0, keepdims=True)
    src_t = (acs - jnp.log(dt)).T

    e_in = jnp.exp(acs).astype(BF16)
    e_out = (jnp.exp(tot - acs) * dt).astype(BF16)
    e_tot = jnp.broadcast_to(jnp.exp(tot), (8, SSD_HEADS)).astype(BF16)
    zero = jnp.zeros((CHUNK, LANES), BF16)

    def prep(g):
        gl = slice(g * GROUP_WIDTH, (g + 1) * GROUP_WIDTH)
        bg = bm_ref[:, g * D_STATE:(g + 1) * D_STATE]
        cg = cm_ref[:, g * D_STATE:(g + 1) * D_STATE]
        cb = lax.dot_general(cg, bg, (((1,), (1,)), ((), ())), preferred_element_type=F32)
        y_raw = jnp.dot(cg, state_ref[:, gl].astype(BF16), preferred_element_type=F32)
        e = e_ref[:, gl]
        decay_in = jnp.dot(e_in, e, preferred_element_type=F32)
        decay_out = jnp.dot(e_out, e, preferred_element_type=F32).astype(BF16)
        decay_tot = jnp.dot(e_tot, e, preferred_element_type=F32)[0:1]
        return bg, cb, y_raw * decay_in, decay_out, decay_tot

    ready = prep(0)
    yield
    for g in range(N_GROUPS):
        gl = slice(g * GROUP_WIDTH, (g + 1) * GROUP_WIDTH)
        bg, cb, y_state, decay_out, decay_tot = ready
        if g + 1 < N_GROUPS:
            ready = prep(g + 1)
        ys = []
        for pair in range(HEADS_PER_GROUP // 2):
            ms = []
            for h in (g * HEADS_PER_GROUP + 2 * pair, g * HEADS_PER_GROUP + 2 * pair + 1):
                seg = acs[:, h:h + 1] - src_t[h:h + 1, :]
                ms.append((cb * jnp.exp(jnp.where(mask, seg, -jnp.inf))).astype(BF16))
            xp = xs_ref[:, g * GROUP_WIDTH + pair * LANES:g * GROUP_WIDTH + (pair + 1) * LANES]
            x_diag = jnp.concatenate([jnp.where(low_half, xp, zero), jnp.where(low_half, zero, xp)], axis=0)
            ys.append(jnp.dot(jnp.concatenate(ms, axis=1), x_diag, preferred_element_type=F32))
        y_ref[:, gl] = (jnp.concatenate(ys, axis=1) + y_state).astype(y_ref.dtype)
        xw = xs_ref[:, gl] * decay_out
        bg_t = bg.astype(F32).T.astype(BF16)
        state_ref[:, gl] = state_ref[:, gl] * decay_tot + jnp.dot(bg_t, xw, preferred_element_type=F32)
        yield


def _scan_kernel(xs_f, bm_f, cm_f, dt_f, xs_b, bm_b, cm_b, dt_b, alog_ref, e_ref, y_f, y_b, state_f, state_b):
    @pl.when(pl.program_id(1) == 0)
    def _():
        state_f[...] = jnp.zeros_like(state_f)
        state_b[...] = jnp.zeros_like(state_b)

    streams = [_scan_direction(xs_f, bm_f, cm_f, dt_f, alog_ref[0], e_ref, y_f, state_f, False),
               _scan_direction(xs_b, bm_b, cm_b, dt_b, alog_ref[1], e_ref, y_b, state_b, True)]
    for _ in range(N_GROUPS + 1):
        for stream in streams:
            next(stream)


def _scan(xbc, dt2, alog, batch, seq):
    t = xbc.shape[0]
    nc = seq // CHUNK
    heads = jnp.arange(D_INNER, dtype=jnp.int32) // SSD_HEAD_DIM
    expand = (heads[None, :] == jnp.arange(SSD_HEADS, dtype=jnp.int32)[:, None]).astype(BF16)
    b_block = D_INNER // BC_DIM
    fwd = lambda b, c: b * nc + c
    bwd = lambda b, c: b * nc + nc - 1 - c

    def chunk_specs(chunk, direction):
        return [pl.BlockSpec((CHUNK, D_INNER), lambda b, c: (chunk(b, c), 0)),
                pl.BlockSpec((CHUNK, BC_DIM), lambda b, c: (chunk(b, c), b_block)),
                pl.BlockSpec((CHUNK, BC_DIM), lambda b, c: (chunk(b, c), b_block + 1)),
                pl.BlockSpec((1, CHUNK, SSD_HEADS), lambda b, c: (direction, chunk(b, c), 0))]

    y_shape = jax.ShapeDtypeStruct((t, D_INNER), BF16)
    return pl.pallas_call(
        _scan_kernel,
        grid=(batch, nc),
        in_specs=chunk_specs(fwd, 0) + chunk_specs(bwd, 1)
        + [pl.BlockSpec((2, 1, SSD_HEADS), lambda b, c: (0, 0, 0)),
           pl.BlockSpec((SSD_HEADS, D_INNER), lambda b, c: (0, 0))],
        out_specs=[pl.BlockSpec((CHUNK, D_INNER), lambda b, c: (fwd(b, c), 0)),
                   pl.BlockSpec((CHUNK, D_INNER), lambda b, c: (bwd(b, c), 0))],
        out_shape=[y_shape, y_shape],
        scratch_shapes=[pltpu.VMEM((D_STATE, D_INNER), F32), pltpu.VMEM((D_STATE, D_INNER), F32)],
        compiler_params=_params(("parallel", "arbitrary"), 48),
        name="ssd_scan",
    )(xbc, xbc, xbc, dt2, xbc, xbc, xbc, dt2, alog, expand)


def _ssd_out_kernel(yf_ref, yb_ref, xs_ref, z_ref, x_ref, p_ref, dskip_ref, nw_ref, wout_ref,
                    lng_ref, lnb_ref, wpe_ref, wpg_ref, out_ref):
    tm = x_ref.shape[0]
    sum_sq = jnp.zeros((tm, 1), F32)
    h = jnp.zeros((tm, D_MODEL), F32)
    for kc in range(D_INNER // OUT_CHUNK):
        cols = slice(kc * OUT_CHUNK, (kc + 1) * OUT_CHUNK)
        y = (yf_ref[:, cols].astype(F32) + yb_ref[:, cols].astype(F32)
             + xs_ref[:, cols].astype(F32) * dskip_ref[:, cols])
        u = y * _silu(z_ref[:, cols].astype(F32))
        sum_sq = sum_sq + jnp.sum(u * u, axis=-1, keepdims=True)
        h = h + jnp.dot((u * nw_ref[:, cols]).astype(BF16), wout_ref[cols, :], preferred_element_type=F32)
    h = h * lax.rsqrt(sum_sq * (1.0 / D_INNER) + RMS_EPS)
    out_ref[...] = _deepnorm_ple(x_ref[...], h, p_ref[...], lng_ref[...], lnb_ref[...], wpe_ref[...], wpg_ref[...])


OUT_CHUNK = 1024


def _ssd_out(y_f, y_b, xbc, zx, x2, p1, dskip, nw, wout, lng, lnb, wpe, wpg, tm=256):
    t = x2.shape[0]
    row = lambda w: pl.BlockSpec((tm, w), lambda i: (i, 0))
    return pl.pallas_call(
        _ssd_out_kernel,
        grid=(t // tm,),
        in_specs=[row(D_INNER), row(D_INNER), row(D_INNER), row(D_INNER), row(D_MODEL), row(PLE_DIM),
                  _resident((1, D_INNER)), _resident((1, D_INNER)), _resident((D_INNER, D_MODEL)),
                  _resident((1, D_MODEL)), _resident((1, D_MODEL)),
                  _resident((PLE_DIM, D_MODEL)), _resident((D_MODEL, D_MODEL))],
        out_specs=row(D_MODEL),
        out_shape=jax.ShapeDtypeStruct((t, D_MODEL), F32),
        compiler_params=_params(("parallel",), 56),
        name="ssd_out",
    )(y_f, y_b, xbc, zx, x2, p1, dskip, nw, wout, lng, lnb, wpe, wpg)


def kernel(x, p, positions, ln_g, ln_b, ple_w_proj, ple_w_gate, mla_w_in, mla_q_norm, mla_kv_norm, mla_w_uq,
           mla_w_ukv, mla_w_o, ssd_w_in, ssd_conv_w, ssd_conv_b, ssd_dt_bias, ssd_a_log, ssd_d, ssd_norm, ssd_w_out):
    batch, seq, _ = x.shape
    t = batch * seq
    x2 = x.reshape(t, D_MODEL)
    p2 = p.reshape(DEPTH, t, PLE_DIM)
    row = lambda v: v.reshape(1, -1)

    inv_freq = 1.0 / (ROPE_BASE ** (jnp.arange(0, QK_ROPE, 2, dtype=F32) / QK_ROPE))
    freq_row = jnp.concatenate([inv_freq, inv_freq, jnp.zeros((ROPE_PAD - QK_ROPE,), F32)]).reshape(1, ROPE_PAD)
    ct, sa, sb, cos_t, sin_t = _rope_tables(positions.astype(F32).reshape(t, 1), freq_row)

    lat = Q_LORA + KV_LORA
    w_in = mla_w_in[0]
    w_in_p = jnp.concatenate([w_in[:, :lat + QK_ROPE], jnp.zeros((D_MODEL, ROPE_PAD - QK_ROPE), F32),
                              w_in[:, lat + QK_ROPE:]], axis=1).astype(BF16)
    wuqt_p = jnp.pad(mla_w_uq[0].astype(BF16).reshape(Q_LORA, MLA_HEADS, QK_HEAD).transpose(1, 2, 0),
                     ((0, 0), (0, QK_PAD - QK_HEAD), (0, 0)))
    wukv_p = mla_w_ukv[0].reshape(KV_LORA, MLA_HEADS, QK_NOPE + V_HEAD).transpose(1, 0, 2).astype(BF16)

    cqt, ckvn, kr, gate = _mla_in(x2, w_in_p, row(mla_q_norm[0]), row(mla_kv_norm[0]), ct, sa, sb)
    qt, k, vt = _mla_up(cqt, ckvn, kr, cos_t, sin_t, wuqt_p, wukv_p, batch, seq)
    o = _attention(qt, k, vt).reshape(t, MLA_WIDTH)
    x2 = _mla_out(o, gate, x2, p2[0], mla_w_o[0].astype(BF16), row(ln_g[0]), row(ln_b[0]),
                  ple_w_proj[0].astype(BF16), ple_w_gate[0].astype(BF16))

    w_ssd = ssd_w_in[0].astype(BF16)
    zx = _proj(x2, w_ssd, D_INNER + CONV_DIM, BF16)
    dt = _dt_proj(x2, w_ssd, (D_INNER + CONV_DIM) // (2 * SSD_HEADS), ssd_dt_bias[0].reshape(1, 2 * SSD_HEADS))
    xbc = _conv(zx, ssd_conv_w[0], row(ssd_conv_b[0]), batch, seq)
    dt2 = dt.reshape(t, 2, SSD_HEADS).transpose(1, 0, 2)
    y_f, y_b = _scan(xbc, dt2, ssd_a_log[0].reshape(2, 1, SSD_HEADS), batch, seq)
    dskip = jnp.repeat(ssd_d[0], SSD_HEAD_DIM).reshape(1, D_INNER)
    x2 = _ssd_out(y_f, y_b, xbc, zx, x2, p2[1], dskip, row(ssd_norm[0]), ssd_w_out[0].astype(BF16),
                  row(ln_g[1]), row(ln_b[1]), ple_w_proj[1].astype(BF16), ple_w_gate[1].astype(BF16))
    return x2.reshape(batch, seq, D_MODEL)
```

```python
import functools

import jax
import jax.numpy as jnp
from jax import lax
from jax.experimental import pallas as pl
from jax.experimental.pallas import tpu as pltpu

F32 = jnp.float32
BF16 = jnp.bfloat16

D_MODEL = 2048
DEPTH = 2
ALPHA = (2 * DEPTH) ** 0.25
PLE_DIM = 256

MLA_HEADS = 16
Q_LORA = 512
KV_LORA = 512
QK_NOPE = 128
QK_ROPE = 64
V_HEAD = 128
QK_HEAD = QK_NOPE + QK_ROPE
MLA_WIDTH = MLA_HEADS * V_HEAD
ROPE_BASE = 10000.0
ATTN_SCALE = QK_HEAD ** -0.5
LOG2_E = 1.4426950408889634
QK_PAD = 256
ROPE_PAD = 128

D_INNER = 4096
SSD_HEAD_DIM = 64
SSD_HEADS = 64
N_GROUPS = 8
HEADS_PER_GROUP = 8
D_STATE = 128
D_CONV = 7
CONV_PAD = D_CONV // 2
GROUP_WIDTH = HEADS_PER_GROUP * SSD_HEAD_DIM
BC_DIM = N_GROUPS * D_STATE
CONV_DIM = D_INNER + 2 * BC_DIM
CHUNK = 128
LANES = 128

LN_EPS = 1e-5
RMS_EPS = 1e-6

MIB = 1024 * 1024


def _params(semantics, vmem_mib):
    return pltpu.CompilerParams(dimension_semantics=semantics, vmem_limit_bytes=vmem_mib * MIB)


def _resident(shape, index=None):
    index = (0,) * len(shape) if index is None else index
    return pl.BlockSpec(shape, lambda *_: index, pipeline_mode=pl.Buffered(1))


def _rmsnorm(v, g):
    return v * lax.rsqrt(jnp.mean(v * v, axis=-1, keepdims=True) + RMS_EPS) * g


def _layernorm(v, g, b):
    mu = jnp.mean(v, axis=-1, keepdims=True)
    d = v - mu
    var = jnp.mean(d * d, axis=-1, keepdims=True)
    return d * lax.rsqrt(var + LN_EPS) * g + b


def _silu(v):
    return v * jax.nn.sigmoid(v)


def _rope(u, ct, sa, sb):
    return u * ct + pltpu.roll(u, QK_ROPE // 2, 1) * sa + pltpu.roll(u, ROPE_PAD - QK_ROPE // 2, 1) * sb


def _rope_table_kernel(pos_ref, freq_ref, ct_ref, sa_ref, sb_ref, cos_t_ref, sin_t_ref):
    ang = pos_ref[...] * freq_ref[...]
    c = jnp.cos(ang)
    s = jnp.sin(ang)
    cos_t_ref[...] = c
    sin_t_ref[...] = s
    zero = jnp.zeros_like(c)
    pad = jnp.zeros((ROPE_PAD - QK_ROPE, c.shape[1]), F32)
    ct_ref[...] = jnp.concatenate([c, c, pad], axis=0).T
    sa_ref[...] = jnp.concatenate([zero, s, pad], axis=0).T
    sb_ref[...] = jnp.concatenate([-s, zero, pad], axis=0).T


def _rope_tables(pos_row, freq_col, tm=2048):
    t = pos_row.shape[1]
    tab = jax.ShapeDtypeStruct((t, ROPE_PAD), F32)
    tab_t = jax.ShapeDtypeStruct((QK_ROPE // 2, t), F32)
    row = pl.BlockSpec((tm, ROPE_PAD), lambda i: (i, 0))
    col = pl.BlockSpec((QK_ROPE // 2, tm), lambda i: (0, i))
    return pl.pallas_call(
        _rope_table_kernel,
        grid=(t // tm,),
        in_specs=[pl.BlockSpec((1, tm), lambda i: (0, i)), pl.BlockSpec((QK_ROPE // 2, 1), lambda i: (0, 0))],
        out_specs=[row, row, row, col, col],
        out_shape=[tab, tab, tab, tab_t, tab_t],
        compiler_params=_params(("parallel",), 32),
        name="rope_tables",
    )(pos_row, freq_col)


def _mla_in_kernel(x_ref, wlat_ref, wz_ref, qn_ref, kvn_ref, ct_ref, sa_ref, sb_ref, cqt_ref, ckv_ref, kr_ref,
                   g_ref):
    xb = x_ref[...].astype(BF16)
    h = jnp.dot(xb, wlat_ref[...], preferred_element_type=F32)
    cqt_ref[...] = _rmsnorm(h[:, :Q_LORA], qn_ref[...]).T.astype(BF16)
    ckv_ref[...] = _rmsnorm(h[:, Q_LORA:Q_LORA + KV_LORA], kvn_ref[...]).astype(BF16)
    kr_ref[...] = _rope(h[:, Q_LORA + KV_LORA:], ct_ref[...], sa_ref[...], sb_ref[...]).astype(BF16)
    z = jnp.dot(xb, wz_ref[...], preferred_element_type=F32)
    g_ref[...] = _silu(z).astype(BF16)


def _mla_in(x2, w_lat, w_z, qn, kvn, ct, sa, sb, tm=512):
    t = x2.shape[0]
    row = lambda w: pl.BlockSpec((tm, w), lambda i: (i, 0))
    return pl.pallas_call(
        _mla_in_kernel,
        grid=(t // tm,),
        in_specs=[row(D_MODEL), _resident(w_lat.shape), _resident(w_z.shape), _resident((1, Q_LORA)),
                  _resident((1, KV_LORA)),
                  row(ROPE_PAD), row(ROPE_PAD), row(ROPE_PAD)],
        out_specs=[pl.BlockSpec((Q_LORA, tm), lambda i: (0, i)), row(KV_LORA), row(ROPE_PAD), row(MLA_WIDTH)],
        out_shape=[jax.ShapeDtypeStruct((Q_LORA, t), BF16), jax.ShapeDtypeStruct((t, KV_LORA), BF16),
                   jax.ShapeDtypeStruct((t, ROPE_PAD), BF16), jax.ShapeDtypeStruct((t, MLA_WIDTH), BF16)],
        compiler_params=_params(("parallel",), 48),
        name="mla_in",
    )(x2, w_lat, w_z, qn, kvn, ct, sa, sb)


UP_HEADS = 4


def _mla_up_kernel(cqt_ref, ckv_ref, kr_ref, cos_ref, sin_ref, wuqt_ref, wukv_ref, qt_ref, k_ref, vt_ref):
    half = QK_ROPE // 2
    c = cos_ref[...]
    s = sin_ref[...]
    for j in range(UP_HEADS):
        qt = jnp.dot(wuqt_ref[j], cqt_ref[...], preferred_element_type=F32) * (ATTN_SCALE * LOG2_E)
        t1 = qt[QK_NOPE:QK_NOPE + half]
        t2 = qt[QK_NOPE + half:QK_HEAD]
        qt_ref[0, j, :QK_NOPE, :] = qt[:QK_NOPE].astype(BF16)
        qt_ref[0, j, QK_NOPE:QK_NOPE + half, :] = (t1 * c - t2 * s).astype(BF16)
        qt_ref[0, j, QK_NOPE + half:QK_HEAD, :] = (t2 * c + t1 * s).astype(BF16)
        qt_ref[0, j, QK_HEAD:, :] = jnp.zeros((QK_PAD - QK_HEAD, qt.shape[1]), BF16)
        kv = jnp.dot(ckv_ref[...], wukv_ref[j], preferred_element_type=F32)
        k_ref[0, j, :, :QK_NOPE] = kv[:, :QK_NOPE].astype(BF16)
        k_ref[0, j, :, QK_NOPE:] = kr_ref[...]
        vt_ref[0, j] = kv[:, QK_NOPE:].T.astype(BF16)


def _mla_up(cqt, ckvn, kr, cos_t, sin_t, wuqt_p, wukv_p, batch, seq, tm=1024):
    nt = seq // tm
    row = lambda w: pl.BlockSpec((tm, w), lambda b, i, h: (b * nt + i, 0))
    col = lambda w: pl.BlockSpec((w, tm), lambda b, i, h: (0, b * nt + i))
    feat = lambda w: pl.BlockSpec((1, UP_HEADS, w, tm), lambda b, i, h: (b, h, 0, i))
    return pl.pallas_call(
        _mla_up_kernel,
        grid=(batch, nt, MLA_HEADS // UP_HEADS),
        in_specs=[col(Q_LORA), row(KV_LORA), row(ROPE_PAD), col(QK_ROPE // 2), col(QK_ROPE // 2),
                  pl.BlockSpec((UP_HEADS, QK_PAD, Q_LORA), lambda b, i, h: (h, 0, 0)),
                  pl.BlockSpec((UP_HEADS, KV_LORA, QK_PAD), lambda b, i, h: (h, 0, 0))],
        out_specs=[feat(QK_PAD), pl.BlockSpec((1, UP_HEADS, tm, QK_PAD), lambda b, i, h: (b, h, i, 0)),
                   feat(V_HEAD)],
        out_shape=[jax.ShapeDtypeStruct((batch, MLA_HEADS, QK_PAD, seq), BF16),
                   jax.ShapeDtypeStruct((batch, MLA_HEADS, seq, QK_PAD), BF16),
                   jax.ShapeDtypeStruct((batch, MLA_HEADS, V_HEAD, seq), BF16)],
        compiler_params=_params(("parallel", "parallel", "arbitrary"), 40),
        name="mla_up",
    )(cqt, ckvn, kr, cos_t, sin_t, wuqt_p, wukv_p)


def _attn_kernel(qt_ref, k_ref, vt_ref, o_ref, s_ref, *, tk, n_col):
    tq = qt_ref.shape[3]
    w = tq // n_col
    n_chunks = k_ref.shape[2] // tk

    def scores(c, slot, g):
        s = jnp.dot(k_ref[0, 0, pl.ds(c * tk, tk), :], qt_ref[0, 0, :, g * w:(g + 1) * w],
                    preferred_element_type=F32)
        s_ref[slot, :, g * w:(g + 1) * w] = s
        return jnp.max(s, axis=0, keepdims=True)

    def consume(c, slot, g, m, m_chunk, l, acc):
        m_new = jnp.maximum(m, m_chunk)
        alpha = jnp.exp2(m - m_new)
        p = jnp.exp2(s_ref[slot, :, g * w:(g + 1) * w] - m_new)
        l = alpha * l + jnp.sum(p.reshape(tk // 8, 8, w), axis=0)
        pv = jnp.dot(vt_ref[0, 0, :, pl.ds(c * tk, tk)], p.astype(BF16), preferred_element_type=F32)
        return m_new, l, alpha * acc + pv

    carry = [(jnp.full((1, w), -jnp.inf, F32), scores(0, 0, g), jnp.zeros((8, w), F32),
              jnp.zeros((V_HEAD, w), F32)) for g in range(n_col)]
    for c in range(n_chunks):
        slot = c % 2
        for g in range(n_col):
            m, m_chunk, l, acc = carry[g]
            m_next = scores(c + 1, 1 - slot, g) if c + 1 < n_chunks else m_chunk
            m, l, acc = consume(c, slot, g, m, m_chunk, l, acc)
            carry[g] = (m, m_next, l, acc)
    for g in range(n_col):
        _, _, l, acc = carry[g]
        o_ref[0, g * w:(g + 1) * w, :] = (acc / jnp.sum(l, axis=0, keepdims=True)).T.astype(o_ref.dtype)


def _attention(qt, k, vt, tq=1024, tk=512, n_col=2):
    batch, heads, seq, _ = k.shape
    assert seq % (2 * tk) == 0 and seq % tq == 0
    return pl.pallas_call(
        functools.partial(_attn_kernel, tk=tk, n_col=n_col),
        grid=(batch, heads, seq // tq),
        in_specs=[pl.BlockSpec((1, 1, QK_PAD, tq), lambda b, h, i: (b, h, 0, i)),
                  pl.BlockSpec((1, 1, seq, QK_PAD), lambda b, h, i: (b, h, 0, 0)),
                  pl.BlockSpec((1, 1, V_HEAD, seq), lambda b, h, i: (b, h, 0, 0))],
        out_specs=pl.BlockSpec((1, tq, V_HEAD), lambda b, h, i: (b, i, h)),
        out_shape=jax.ShapeDtypeStruct((batch, seq, heads * V_HEAD), BF16),
        scratch_shapes=[pltpu.VMEM((2, tk, tq), F32)],
        compiler_params=_params(("parallel", "parallel", "arbitrary"), 48),
        name="mla_attention",
    )(qt, k, vt)


def _deepnorm_ple(x, h, p, lng, lnb, wpe, wpg):
    y = _layernorm(ALPHA * x + h, lng, lnb)
    emb = jnp.dot(p.astype(BF16), wpe, preferred_element_type=F32)
    gate = jax.nn.sigmoid(jnp.dot(y.astype(BF16), wpg, preferred_element_type=F32))
    return y + emb * gate


def _mla_out_kernel(o_ref, g_ref, x_ref, p_ref, wo_ref, lng_ref, lnb_ref, wpe_ref, wpg_ref, out_ref):
    og = (o_ref[...].astype(F32) * g_ref[...].astype(F32)).astype(BF16)
    h = jnp.dot(og, wo_ref[...], preferred_element_type=F32)
    out_ref[...] = _deepnorm_ple(x_ref[...], h, p_ref[...], lng_ref[...], lnb_ref[...], wpe_ref[...], wpg_ref[...])


def _mla_out(o, g, x2, p0, wo, lng, lnb, wpe, wpg, tm=512):
    t = x2.shape[0]
    row = lambda w: pl.BlockSpec((tm, w), lambda i: (i, 0))
    return pl.pallas_call(
        _mla_out_kernel,
        grid=(t // tm,),
        in_specs=[row(MLA_WIDTH), row(MLA_WIDTH), row(D_MODEL), row(PLE_DIM),
                  _resident((MLA_WIDTH, D_MODEL)), _resident((1, D_MODEL)), _resident((1, D_MODEL)),
                  _resident((PLE_DIM, D_MODEL)), _resident((D_MODEL, D_MODEL))],
        out_specs=row(D_MODEL),
        out_shape=jax.ShapeDtypeStruct((t, D_MODEL), F32),
        compiler_params=_params(("parallel",), 56),
        name="mla_out",
    )(o, g, x2, p0, wo, lng, lnb, wpe, wpg)


def _proj_kernel(x_ref, w_ref, o_ref, xb_ref):
    @pl.when(pl.program_id(1) == 0)
    def _():
        xb_ref[...] = x_ref[...].astype(BF16)

    o_ref[...] = jnp.dot(xb_ref[...], w_ref[...], preferred_element_type=F32).astype(o_ref.dtype)


def _proj(x2, w, n, out_dtype, tm=1024, tn=1024):
    t, kdim = x2.shape
    return pl.pallas_call(
        _proj_kernel,
        grid=(t // tm, n // tn),
        in_specs=[pl.BlockSpec((tm, kdim), lambda i, j: (i, 0)), pl.BlockSpec((kdim, tn), lambda i, j: (0, j))],
        out_specs=pl.BlockSpec((tm, tn), lambda i, j: (i, j)),
        out_shape=jax.ShapeDtypeStruct((t, n), out_dtype),
        scratch_shapes=[pltpu.VMEM((tm, kdim), BF16)],
        compiler_params=_params(("parallel", "arbitrary"), 48),
        name="ssd_in_proj",
    )(x2, w)


def _dt_kernel(x_ref, w_ref, bias_ref, o_ref):
    h = jnp.dot(x_ref[...].astype(BF16), w_ref[...], preferred_element_type=F32) + bias_ref[...]
    o_ref[...] = jnp.maximum(h, 0.0) + jnp.log1p(jnp.exp(-jnp.abs(h)))


def _dt_proj(x2, w, col_block, bias_row, tm=1024):
    t, kdim = x2.shape
    n = bias_row.shape[1]
    return pl.pallas_call(
        _dt_kernel,
        grid=(t // tm,),
        in_specs=[pl.BlockSpec((tm, kdim), lambda i: (i, 0)), _resident((kdim, n), (0, col_block)), _resident((1, n))],
        out_specs=pl.BlockSpec((tm, n), lambda i: (i, 0)),
        out_shape=jax.ShapeDtypeStruct((t, n), F32),
        compiler_params=_params(("parallel",), 32),
        name="ssd_dt_proj",
    )(x2, w, bias_row)


CONV_HALO = 16
CONV_COLS = 2048
CONV_ROWS = 64


def _conv_kernel(prev_ref, cur_ref, next_ref, w_ref, b_ref, out_ref, ext_ref):
    i = pl.program_id(1)
    tc = cur_ref.shape[0]
    has_prev = (i > 0).astype(F32)
    has_next = (i < pl.num_programs(1) - 1).astype(F32)
    for ct in range(CONV_COLS // LANES):
        lanes = slice(ct * LANES, (ct + 1) * LANES)
        ext_ref[ct, 0:CONV_HALO, :] = prev_ref[:, lanes].astype(F32) * has_prev
        ext_ref[ct, CONV_HALO:CONV_HALO + tc, :] = cur_ref[:, lanes].astype(F32)
        ext_ref[ct, CONV_HALO + tc:, :] = next_ref[:, lanes].astype(F32) * has_next
    for ct in range(CONV_COLS // LANES):
        lanes = slice(ct * LANES, (ct + 1) * LANES)
        taps = [w_ref[k:k + 1, lanes] for k in range(D_CONV)]
        bias = b_ref[:, lanes]
        for rb in range(tc // CONV_ROWS):
            base = CONV_HALO - CONV_PAD + rb * CONV_ROWS
            acc = bias + ext_ref[ct, base:base + CONV_ROWS, :] * taps[0]
            for k in range(1, D_CONV):
                acc = acc + ext_ref[ct, base + k:base + k + CONV_ROWS, :] * taps[k]
            out_ref[rb * CONV_ROWS:(rb + 1) * CONV_ROWS, lanes] = _silu(acc).astype(BF16)


def _conv(zx, conv_w, conv_b, batch, seq, tc=512):
    t = zx.shape[0]
    nt = seq // tc
    per = tc // CONV_HALO
    last_halo = t // CONV_HALO - 1
    skip = D_INNER // CONV_COLS
    return pl.pallas_call(
        _conv_kernel,
        grid=(batch, nt, CONV_DIM // CONV_COLS),
        in_specs=[pl.BlockSpec((CONV_HALO, CONV_COLS),
                               lambda b, i, j: (jnp.maximum((b * nt + i) * per - 1, 0), j + skip)),
                  pl.BlockSpec((tc, CONV_COLS), lambda b, i, j: (b * nt + i, j + skip)),
                  pl.BlockSpec((CONV_HALO, CONV_COLS),
                               lambda b, i, j: (jnp.minimum((b * nt + i + 1) * per, last_halo), j + skip)),
                  pl.BlockSpec((D_CONV, CONV_COLS), lambda b, i, j: (0, j)),
                  pl.BlockSpec((1, CONV_COLS), lambda b, i, j: (0, j))],
        out_specs=pl.BlockSpec((tc, CONV_COLS), lambda b, i, j: (b * nt + i, j)),
        out_shape=jax.ShapeDtypeStruct((t, CONV_DIM), BF16),
        scratch_shapes=[pltpu.VMEM((CONV_COLS // LANES, tc + 2 * CONV_HALO, LANES), F32)],
        compiler_params=_params(("parallel", "parallel", "parallel"), 32),
        name="ssd_conv",
    )(zx, zx, zx, conv_w, conv_b)


def _split3(v):
    hi = v.astype(BF16)
    r = v - hi.astype(F32)
    mid = r.astype(BF16)
    lo = (r - mid.astype(F32)).astype(BF16)
    return hi, mid, lo


def _scan_direction(xs_ref, bm_ref, cm_ref, dt_ref, alog, e_ref, y_ref, state_ref, reverse):
    row = lax.broadcasted_iota(jnp.int32, (CHUNK, CHUNK), 0)
    col = lax.broadcasted_iota(jnp.int32, (CHUNK, CHUNK), 1)
    mask = (col >= row) if reverse else (row >= col)
    cum = mask.astype(BF16)
    low_half = col < SSD_HEAD_DIM

    dt = dt_ref[0]
    a = dt * -jnp.exp(alog)
    a_hi, a_mid, a_lo = _split3(a)
    acs = (jnp.dot(cum, a_hi, preferred_element_type=F32) + jnp.dot(cum, a_mid, preferred_element_type=F32)
           + jnp.dot(cum, a_lo, preferred_element_type=F32))
    tot = jnp.sum(a, axis=0, keepdims=True)
    src_t = (acs - jnp.log(dt)).T

    e_in = jnp.exp(acs).astype(BF16)
    e_out = (jnp.exp(tot - acs) * dt).astype(BF16)
    e_tot = jnp.broadcast_to(jnp.exp(tot), (8, SSD_HEADS)).astype(BF16)
    zero = jnp.zeros((CHUNK, LANES), BF16)

    def prep(g):
        gl = slice(g * GROUP_WIDTH, (g + 1) * GROUP_WIDTH)
        bg = bm_ref[:, g * D_STATE:(g + 1) * D_STATE]
        cg = cm_ref[:, g * D_STATE:(g + 1) * D_STATE]
        cb = lax.dot_general(cg, bg, (((1,), (1,)), ((), ())), preferred_element_type=F32)
        y_raw = jnp.dot(cg, state_ref[:, gl].astype(BF16), preferred_element_type=F32)
        e = e_ref[:, gl]
        decay_in = jnp.dot(e_in, e, preferred_element_type=F32)
        decay_out = jnp.dot(e_out, e, preferred_element_type=F32).astype(BF16)
        decay_tot = jnp.dot(e_tot, e, preferred_element_type=F32)[0:1]
        return bg, cb, y_raw * decay_in, decay_out, decay_tot

    ready = prep(0)
    yield
    for g in range(N_GROUPS):
        gl = slice(g * GROUP_WIDTH, (g + 1) * GROUP_WIDTH)
        bg, cb, y_state, decay_out, decay_tot = ready
        if g + 1 < N_GROUPS:
            ready = prep(g + 1)
        ys = []
        for pair in range(HEADS_PER_GROUP // 2):
            ms = []
            for h in (g * HEADS_PER_GROUP + 2 * pair, g * HEADS_PER_GROUP + 2 * pair + 1):
                seg = acs[:, h:h + 1] - src_t[h:h + 1, :]
                ms.append((cb * jnp.exp(jnp.where(mask, seg, -jnp.inf))).astype(BF16))
            xp = xs_ref[:, g * GROUP_WIDTH + pair * LANES:g * GROUP_WIDTH + (pair + 1) * LANES]
            x_diag = jnp.concatenate([jnp.where(low_half, xp, zero), jnp.where(low_half, zero, xp)], axis=0)
            ys.append(jnp.dot(jnp.concatenate(ms, axis=1), x_diag, preferred_element_type=F32))
        y_ref[:, gl] = (jnp.concatenate(ys, axis=1) + y_state).astype(y_ref.dtype)
        xw = xs_ref[:, gl] * decay_out
        bg_t = bg.astype(F32).T.astype(BF16)
        state_ref[:, gl] = state_ref[:, gl] * decay_tot + jnp.dot(bg_t, xw, preferred_element_type=F32)
        yield


def _scan_kernel(xs_f, bm_f, cm_f, dt_f, xs_b, bm_b, cm_b, dt_b, alog_ref, e_ref, y_f, y_b, state_f, state_b):
    @pl.when(pl.program_id(1) == 0)
    def _():
        state_f[...] = jnp.zeros_like(state_f)
        state_b[...] = jnp.zeros_like(state_b)

    streams = [_scan_direction(xs_f, bm_f, cm_f, dt_f, alog_ref[0], e_ref, y_f, state_f, False),
               _scan_direction(xs_b, bm_b, cm_b, dt_b, alog_ref[1], e_ref, y_b, state_b, True)]
    for _ in range(N_GROUPS + 1):
        for stream in streams:
            next(stream)


def _scan(xbc, dt2, alog, batch, seq):
    t = xbc.shape[0]
    nc = seq // CHUNK
    heads = jnp.arange(D_INNER, dtype=jnp.int32) // SSD_HEAD_DIM
    expand = (heads[None, :] == jnp.arange(SSD_HEADS, dtype=jnp.int32)[:, None]).astype(BF16)
    b_block = D_INNER // BC_DIM
    fwd = lambda b, c: b * nc + c
    bwd = lambda b, c: b * nc + nc - 1 - c

    def chunk_specs(chunk, direction):
        return [pl.BlockSpec((CHUNK, D_INNER), lambda b, c: (chunk(b, c), 0)),
                pl.BlockSpec((CHUNK, BC_DIM), lambda b, c: (chunk(b, c), b_block)),
                pl.BlockSpec((CHUNK, BC_DIM), lambda b, c: (chunk(b, c), b_block + 1)),
                pl.BlockSpec((1, CHUNK, SSD_HEADS), lambda b, c: (direction, chunk(b, c), 0))]

    y_shape = jax.ShapeDtypeStruct((t, D_INNER), BF16)
    return pl.pallas_call(
        _scan_kernel,
        grid=(batch, nc),
        in_specs=chunk_specs(fwd, 0) + chunk_specs(bwd, 1)
        + [pl.BlockSpec((2, 1, SSD_HEADS), lambda b, c: (0, 0, 0)),
           pl.BlockSpec((SSD_HEADS, D_INNER), lambda b, c: (0, 0))],
        out_specs=[pl.BlockSpec((CHUNK, D_INNER), lambda b, c: (fwd(b, c), 0)),
                   pl.BlockSpec((CHUNK, D_INNER), lambda b, c: (bwd(b, c), 0))],
        out_shape=[y_shape, y_shape],
        scratch_shapes=[pltpu.VMEM((D_STATE, D_INNER), F32), pltpu.VMEM((D_STATE, D_INNER), F32)],
        compiler_params=_params(("parallel", "arbitrary"), 48),
        name="ssd_scan",
    )(xbc, xbc, xbc, dt2, xbc, xbc, xbc, dt2, alog, expand)


def _ssd_out_kernel(yf_ref, yb_ref, xs_ref, z_ref, x_ref, p_ref, dskip_ref, nw_ref, wout_ref,
                    lng_ref, lnb_ref, wpe_ref, wpg_ref, out_ref):
    tm = x_ref.shape[0]
    sum_sq = jnp.zeros((tm, 1), F32)
    h = jnp.zeros((tm, D_MODEL), F32)
    for kc in range(D_INNER // OUT_CHUNK):
        cols = slice(kc * OUT_CHUNK, (kc + 1) * OUT_CHUNK)
        y = (yf_ref[:, cols].astype(F32) + yb_ref[:, cols].astype(F32)
             + xs_ref[:, cols].astype(F32) * dskip_ref[:, cols])
        u = y * _silu(z_ref[:, cols].astype(F32))
        sum_sq = sum_sq + jnp.sum(u * u, axis=-1, keepdims=True)
        h = h + jnp.dot((u * nw_ref[:, cols]).astype(BF16), wout_ref[cols, :], preferred_element_type=F32)
    h = h * lax.rsqrt(sum_sq * (1.0 / D_INNER) + RMS_EPS)
    out_ref[...] = _deepnorm_ple(x_ref[...], h, p_ref[...], lng_ref[...], lnb_ref[...], wpe_ref[...], wpg_ref[...])


OUT_CHUNK = 1024


def _ssd_out(y_f, y_b, xbc, zx, x2, p1, dskip, nw, wout, lng, lnb, wpe, wpg, tm=256):
    t = x2.shape[0]
    row = lambda w: pl.BlockSpec((tm, w), lambda i: (i, 0))
    return pl.pallas_call(
        _ssd_out_kernel,
        grid=(t // tm,),
        in_specs=[row(D_INNER), row(D_INNER), row(D_INNER), row(D_INNER), row(D_MODEL), row(PLE_DIM),
                  _resident((1, D_INNER)), _resident((1, D_INNER)), _resident((D_INNER, D_MODEL)),
                  _resident((1, D_MODEL)), _resident((1, D_MODEL)),
                  _resident((PLE_DIM, D_MODEL)), _resident((D_MODEL, D_MODEL))],
        out_specs=row(D_MODEL),
        out_shape=jax.ShapeDtypeStruct((t, D_MODEL), F32),
        compiler_params=_params(("parallel",), 56),
        name="ssd_out",
    )(y_f, y_b, xbc, zx, x2, p1, dskip, nw, wout, lng, lnb, wpe, wpg)


def kernel(x, p, positions, ln_g, ln_b, ple_w_proj, ple_w_gate, mla_w_in, mla_q_norm, mla_kv_norm, mla_w_uq,
           mla_w_ukv, mla_w_o, ssd_w_in, ssd_conv_w, ssd_conv_b, ssd_dt_bias, ssd_a_log, ssd_d, ssd_norm, ssd_w_out):
    batch, seq, _ = x.shape
    t = batch * seq
    x2 = x.reshape(t, D_MODEL)
    p2 = p.reshape(DEPTH, t, PLE_DIM)
    row = lambda v: v.reshape(1, -1)

    inv_freq = 1.0 / (ROPE_BASE ** (jnp.arange(0, QK_ROPE, 2, dtype=F32) / QK_ROPE))
    ct, sa, sb, cos_t, sin_t = _rope_tables(positions.astype(F32).reshape(1, t), inv_freq.reshape(QK_ROPE // 2, 1))

    lat = Q_LORA + KV_LORA
    w_in = mla_w_in[0]
    w_lat = jnp.pad(w_in[:, :lat + QK_ROPE].astype(BF16), ((0, 0), (0, ROPE_PAD - QK_ROPE)))
    w_z = w_in[:, lat + QK_ROPE:].astype(BF16)
    wuqt_p = jnp.pad(mla_w_uq[0].astype(BF16).reshape(Q_LORA, MLA_HEADS, QK_HEAD).transpose(1, 2, 0),
                     ((0, 0), (0, QK_PAD - QK_HEAD), (0, 0)))
    wukv_p = mla_w_ukv[0].reshape(KV_LORA, MLA_HEADS, QK_NOPE + V_HEAD).transpose(1, 0, 2).astype(BF16)

    cqt, ckvn, kr, gate = _mla_in(x2, w_lat, w_z, row(mla_q_norm[0]), row(mla_kv_norm[0]), ct, sa, sb)
    qt, k, vt = _mla_up(cqt, ckvn, kr, cos_t, sin_t, wuqt_p, wukv_p, batch, seq)
    o = _attention(qt, k, vt).reshape(t, MLA_WIDTH)
    x2 = _mla_out(o, gate, x2, p2[0], mla_w_o[0].astype(BF16), row(ln_g[0]), row(ln_b[0]),
                  ple_w_proj[0].astype(BF16), ple_w_gate[0].astype(BF16))

    w_ssd = ssd_w_in[0].astype(BF16)
    zx = _proj(x2, w_ssd, D_INNER + CONV_DIM, BF16)
    dt = _dt_proj(x2, w_ssd, (D_INNER + CONV_DIM) // (2 * SSD_HEADS), ssd_dt_bias[0].reshape(1, 2 * SSD_HEADS))
    xbc = _conv(zx, ssd_conv_w[0], row(ssd_conv_b[0]), batch, seq)
    dt2 = dt.reshape(t, 2, SSD_HEADS).transpose(1, 0, 2)
    y_f, y_b = _scan(xbc, dt2, ssd_a_log[0].reshape(2, 1, SSD_HEADS), batch, seq)
    dskip = jnp.repeat(ssd_d[0], SSD_HEAD_DIM).reshape(1, D_INNER)
    x2 = _ssd_out(y_f, y_b, xbc, zx, x2, p2[1], dskip, row(ssd_norm[0]), ssd_w_out[0].astype(BF16),
                  row(ln_g[1]), row(ln_b[1]), ple_w_proj[1].astype(BF16), ple_w_gate[1].astype(BF16))
    return x2.reshape(batch, seq, D_MODEL)
```

```python
import functools

import jax
import jax.numpy as jnp
from jax import lax
from jax.experimental import pallas as pl
from jax.experimental.pallas import tpu as pltpu

F32 = jnp.float32
BF16 = jnp.bfloat16

D_MODEL = 2048
DEPTH = 2
ALPHA = (2 * DEPTH) ** 0.25
PLE_DIM = 256

MLA_HEADS = 16
Q_LORA = 512
KV_LORA = 512
QK_NOPE = 128
QK_ROPE = 64
V_HEAD = 128
QK_HEAD = QK_NOPE + QK_ROPE
MLA_WIDTH = MLA_HEADS * V_HEAD
ROPE_BASE = 10000.0
ATTN_SCALE = QK_HEAD ** -0.5
LOG2_E = 1.4426950408889634
QK_PAD = QK_HEAD
ROPE_PAD = 128

D_INNER = 4096
SSD_HEAD_DIM = 64
SSD_HEADS = 64
N_GROUPS = 8
HEADS_PER_GROUP = 8
D_STATE = 128
D_CONV = 7
CONV_PAD = D_CONV // 2
GROUP_WIDTH = HEADS_PER_GROUP * SSD_HEAD_DIM
BC_DIM = N_GROUPS * D_STATE
CONV_DIM = D_INNER + 2 * BC_DIM
CHUNK = 128
LANES = 128

LN_EPS = 1e-5
RMS_EPS = 1e-6

MIB = 1024 * 1024


def _params(semantics, vmem_mib):
    return pltpu.CompilerParams(dimension_semantics=semantics, vmem_limit_bytes=vmem_mib * MIB)


def _resident(shape, index=None):
    index = (0,) * len(shape) if index is None else index
    return pl.BlockSpec(shape, lambda *_: index, pipeline_mode=pl.Buffered(1))


def _rmsnorm(v, g):
    return v * lax.rsqrt(jnp.mean(v * v, axis=-1, keepdims=True) + RMS_EPS) * g


def _layernorm(v, g, b):
    mu = jnp.mean(v, axis=-1, keepdims=True)
    d = v - mu
    var = jnp.mean(d * d, axis=-1, keepdims=True)
    return d * lax.rsqrt(var + LN_EPS) * g + b


def _silu(v):
    return v * jax.nn.sigmoid(v)


def _rope(u, ct, sa, sb):
    return u * ct + pltpu.roll(u, QK_ROPE // 2, 1) * sa + pltpu.roll(u, ROPE_PAD - QK_ROPE // 2, 1) * sb


def _rope_table_kernel(pos_ref, freq_ref, ct_ref, sa_ref, sb_ref, cos_t_ref, sin_t_ref):
    ang = pos_ref[...] * freq_ref[...]
    c = jnp.cos(ang)
    s = jnp.sin(ang)
    cos_t_ref[...] = c
    sin_t_ref[...] = s
    zero = jnp.zeros_like(c)
    pad = jnp.zeros((ROPE_PAD - QK_ROPE, c.shape[1]), F32)
    ct_ref[...] = jnp.concatenate([c, c, pad], axis=0).T
    sa_ref[...] = jnp.concatenate([zero, s, pad], axis=0).T
    sb_ref[...] = jnp.concatenate([-s, zero, pad], axis=0).T


def _rope_tables(pos_row, freq_col, tm=2048):
    t = pos_row.shape[1]
    tab = jax.ShapeDtypeStruct((t, ROPE_PAD), F32)
    tab_t = jax.ShapeDtypeStruct((QK_ROPE // 2, t), F32)
    row = pl.BlockSpec((tm, ROPE_PAD), lambda i: (i, 0))
    col = pl.BlockSpec((QK_ROPE // 2, tm), lambda i: (0, i))
    return pl.pallas_call(
        _rope_table_kernel,
        grid=(t // tm,),
        in_specs=[pl.BlockSpec((1, tm), lambda i: (0, i)), pl.BlockSpec((QK_ROPE // 2, 1), lambda i: (0, 0))],
        out_specs=[row, row, row, col, col],
        out_shape=[tab, tab, tab, tab_t, tab_t],
        compiler_params=_params(("parallel",), 32),
        name="rope_tables",
    )(pos_row, freq_col)


def _mla_in_kernel(x_ref, wlat_ref, wz_ref, qn_ref, kvn_ref, ct_ref, sa_ref, sb_ref, cqt_ref, ckv_ref, kr_ref,
                   g_ref):
    xb = x_ref[...].astype(BF16)
    h = jnp.dot(xb, wlat_ref[...], preferred_element_type=F32)
    cqt_ref[...] = _rmsnorm(h[:, :Q_LORA], qn_ref[...]).T.astype(BF16)
    ckv_ref[...] = _rmsnorm(h[:, Q_LORA:Q_LORA + KV_LORA], kvn_ref[...]).astype(BF16)
    kr_ref[...] = _rope(h[:, Q_LORA + KV_LORA:], ct_ref[...], sa_ref[...], sb_ref[...]).astype(BF16)
    z = jnp.dot(xb, wz_ref[...], preferred_element_type=F32)
    g_ref[...] = _silu(z).astype(BF16)


def _mla_in(x2, w_lat, w_z, qn, kvn, ct, sa, sb, tm=512):
    t = x2.shape[0]
    row = lambda w: pl.BlockSpec((tm, w), lambda i: (i, 0))
    return pl.pallas_call(
        _mla_in_kernel,
        grid=(t // tm,),
        in_specs=[row(D_MODEL), _resident(w_lat.shape), _resident(w_z.shape), _resident((1, Q_LORA)),
                  _resident((1, KV_LORA)),
                  row(ROPE_PAD), row(ROPE_PAD), row(ROPE_PAD)],
        out_specs=[pl.BlockSpec((Q_LORA, tm), lambda i: (0, i)), row(KV_LORA), row(ROPE_PAD), row(MLA_WIDTH)],
        out_shape=[jax.ShapeDtypeStruct((Q_LORA, t), BF16), jax.ShapeDtypeStruct((t, KV_LORA), BF16),
                   jax.ShapeDtypeStruct((t, ROPE_PAD), BF16), jax.ShapeDtypeStruct((t, MLA_WIDTH), BF16)],
        compiler_params=_params(("parallel",), 48),
        name="mla_in",
    )(x2, w_lat, w_z, qn, kvn, ct, sa, sb)


UP_HEADS = 4


def _mla_up_kernel(cqt_ref, ckv_ref, kr_ref, cos_ref, sin_ref, wuqt_ref, wukv_ref, qt_ref, k_ref, vt_ref):
    half = QK_ROPE // 2
    c = cos_ref[...]
    s = sin_ref[...]
    for j in range(UP_HEADS):
        qt = jnp.dot(wuqt_ref[j], cqt_ref[...], preferred_element_type=F32) * (ATTN_SCALE * LOG2_E)
        t1 = qt[QK_NOPE:QK_NOPE + half]
        t2 = qt[QK_NOPE + half:QK_HEAD]
        qt_ref[0, j, :QK_NOPE, :] = qt[:QK_NOPE].astype(BF16)
        qt_ref[0, j, QK_NOPE:QK_NOPE + half, :] = (t1 * c - t2 * s).astype(BF16)
        qt_ref[0, j, QK_NOPE + half:QK_HEAD, :] = (t2 * c + t1 * s).astype(BF16)
        kv = jnp.dot(ckv_ref[...], wukv_ref[j], preferred_element_type=F32)
        k_ref[0, j, :, :QK_NOPE] = kv[:, :QK_NOPE].astype(BF16)
        k_ref[0, j, :, QK_NOPE:] = kr_ref[:, :QK_ROPE]
        vt_ref[0, j] = kv[:, QK_NOPE:].T.astype(BF16)


def _mla_up(cqt, ckvn, kr, cos_t, sin_t, wuqt_p, wukv_p, batch, seq, tm=1024):
    nt = seq // tm
    row = lambda w: pl.BlockSpec((tm, w), lambda b, i, h: (b * nt + i, 0))
    col = lambda w: pl.BlockSpec((w, tm), lambda b, i, h: (0, b * nt + i))
    feat = lambda w: pl.BlockSpec((1, UP_HEADS, w, tm), lambda b, i, h: (b, h, 0, i))
    return pl.pallas_call(
        _mla_up_kernel,
        grid=(batch, nt, MLA_HEADS // UP_HEADS),
        in_specs=[col(Q_LORA), row(KV_LORA), row(ROPE_PAD), col(QK_ROPE // 2), col(QK_ROPE // 2),
                  pl.BlockSpec((UP_HEADS, QK_PAD, Q_LORA), lambda b, i, h: (h, 0, 0)),
                  pl.BlockSpec((UP_HEADS, KV_LORA, QK_NOPE + V_HEAD), lambda b, i, h: (h, 0, 0))],
        out_specs=[feat(QK_PAD), pl.BlockSpec((1, UP_HEADS, tm, QK_PAD), lambda b, i, h: (b, h, i, 0)),
                   feat(V_HEAD)],
        out_shape=[jax.ShapeDtypeStruct((batch, MLA_HEADS, QK_PAD, seq), BF16),
                   jax.ShapeDtypeStruct((batch, MLA_HEADS, seq, QK_PAD), BF16),
                   jax.ShapeDtypeStruct((batch, MLA_HEADS, V_HEAD, seq), BF16)],
        compiler_params=_params(("parallel", "parallel", "arbitrary"), 40),
        name="mla_up",
    )(cqt, ckvn, kr, cos_t, sin_t, wuqt_p, wukv_p)


def _attn_kernel(qt_ref, k_ref, vt_ref, o_ref, s_ref, *, tk, n_col):
    tq = qt_ref.shape[3]
    w = tq // n_col
    n_chunks = k_ref.shape[2] // tk

    def scores(c, slot, g):
        s = jnp.dot(k_ref[0, 0, pl.ds(c * tk, tk), :], qt_ref[0, 0, :, g * w:(g + 1) * w],
                    preferred_element_type=F32)
        s_ref[slot, :, g * w:(g + 1) * w] = s
        return jnp.max(s, axis=0, keepdims=True)

    def consume(c, slot, g, m, m_chunk, l, acc):
        m_new = jnp.maximum(m, m_chunk)
        alpha = jnp.exp2(m - m_new)
        p = jnp.exp2(s_ref[slot, :, g * w:(g + 1) * w] - m_new)
        l = alpha * l + jnp.sum(p.reshape(tk // 8, 8, w), axis=0)
        pv = jnp.dot(vt_ref[0, 0, :, pl.ds(c * tk, tk)], p.astype(BF16), preferred_element_type=F32)
        return m_new, l, alpha * acc + pv

    carry = [(jnp.full((1, w), -jnp.inf, F32), scores(0, 0, g), jnp.zeros((8, w), F32),
              jnp.zeros((V_HEAD, w), F32)) for g in range(n_col)]
    for c in range(n_chunks):
        slot = c % 2
        for g in range(n_col):
            m, m_chunk, l, acc = carry[g]
            m_next = scores(c + 1, 1 - slot, g) if c + 1 < n_chunks else m_chunk
            m, l, acc = consume(c, slot, g, m, m_chunk, l, acc)
            carry[g] = (m, m_next, l, acc)
    for g in range(n_col):
        _, _, l, acc = carry[g]
        o_ref[0, g * w:(g + 1) * w, :] = (acc / jnp.sum(l, axis=0, keepdims=True)).T.astype(o_ref.dtype)


def _attention(qt, k, vt, tq=1024, tk=512, n_col=2):
    batch, heads, seq, _ = k.shape
    assert seq % (2 * tk) == 0 and seq % tq == 0
    return pl.pallas_call(
        functools.partial(_attn_kernel, tk=tk, n_col=n_col),
        grid=(batch, heads, seq // tq),
        in_specs=[pl.BlockSpec((1, 1, QK_PAD, tq), lambda b, h, i: (b, h, 0, i)),
                  pl.BlockSpec((1, 1, seq, QK_PAD), lambda b, h, i: (b, h, 0, 0)),
                  pl.BlockSpec((1, 1, V_HEAD, seq), lambda b, h, i: (b, h, 0, 0))],
        out_specs=pl.BlockSpec((1, tq, V_HEAD), lambda b, h, i: (b, i, h)),
        out_shape=jax.ShapeDtypeStruct((batch, seq, heads * V_HEAD), BF16),
        scratch_shapes=[pltpu.VMEM((2, tk, tq), F32)],
        compiler_params=_params(("parallel", "parallel", "arbitrary"), 48),
        name="mla_attention",
    )(qt, k, vt)


def _ple_rows(layer, tm):
    return pl.BlockSpec((None, tm, PLE_DIM), lambda i: (layer, i, 0))


def _deepnorm_ple(x, h, p, lng, lnb, wpe, wpg):
    y = _layernorm(ALPHA * x + h, lng, lnb)
    emb = jnp.dot(p.astype(BF16), wpe, preferred_element_type=F32)
    gate = jax.nn.sigmoid(jnp.dot(y.astype(BF16), wpg, preferred_element_type=F32))
    return y + emb * gate


def _mla_out_kernel(o_ref, g_ref, x_ref, p_ref, wo_ref, lng_ref, lnb_ref, wpe_ref, wpg_ref, out_ref):
    og = (o_ref[...].astype(F32) * g_ref[...].astype(F32)).astype(BF16)
    h = jnp.dot(og, wo_ref[...], preferred_element_type=F32)
    out_ref[...] = _deepnorm_ple(x_ref[...], h, p_ref[...], lng_ref[...], lnb_ref[...], wpe_ref[...], wpg_ref[...])


def _mla_out(o, g, x2, p2, wo, lng, lnb, wpe, wpg, tm=512):
    t = x2.shape[0]
    row = lambda w: pl.BlockSpec((tm, w), lambda i: (i, 0))
    return pl.pallas_call(
        _mla_out_kernel,
        grid=(t // tm,),
        in_specs=[row(MLA_WIDTH), row(MLA_WIDTH), row(D_MODEL), _ple_rows(0, tm),
                  _resident((MLA_WIDTH, D_MODEL)), _resident((1, D_MODEL)), _resident((1, D_MODEL)),
                  _resident((PLE_DIM, D_MODEL)), _resident((D_MODEL, D_MODEL))],
        out_specs=row(D_MODEL),
        out_shape=jax.ShapeDtypeStruct((t, D_MODEL), F32),
        compiler_params=_params(("parallel",), 56),
        name="mla_out",
    )(o, g, x2, p2, wo, lng, lnb, wpe, wpg)


def _proj_kernel(x_ref, w_ref, o_ref, xb_ref):
    @pl.when(pl.program_id(1) == 0)
    def _():
        xb_ref[...] = x_ref[...].astype(BF16)

    o_ref[...] = jnp.dot(xb_ref[...], w_ref[...], preferred_element_type=F32).astype(o_ref.dtype)


def _proj(x2, w, n, out_dtype, tm=1024, tn=2048):
    t, kdim = x2.shape
    return pl.pallas_call(
        _proj_kernel,
        grid=(t // tm, n // tn),
        in_specs=[pl.BlockSpec((tm, kdim), lambda i, j: (i, 0)), pl.BlockSpec((kdim, tn), lambda i, j: (0, j))],
        out_specs=pl.BlockSpec((tm, tn), lambda i, j: (i, j)),
        out_shape=jax.ShapeDtypeStruct((t, n), out_dtype),
        scratch_shapes=[pltpu.VMEM((tm, kdim), BF16)],
        compiler_params=_params(("parallel", "arbitrary"), 56),
        name="ssd_in_proj",
    )(x2, w)


def _dt_kernel(x_ref, w_ref, bias_ref, o_ref):
    h = jnp.dot(x_ref[...].astype(BF16), w_ref[...], preferred_element_type=F32) + bias_ref[...]
    o_ref[...] = jnp.maximum(h, 0.0) + jnp.log1p(jnp.exp(-jnp.abs(h)))


def _dt_proj(x2, w, col_block, bias_row, tm=1024):
    t, kdim = x2.shape
    n = bias_row.shape[1]
    return pl.pallas_call(
        _dt_kernel,
        grid=(t // tm,),
        in_specs=[pl.BlockSpec((tm, kdim), lambda i: (i, 0)), _resident((kdim, n), (0, col_block)), _resident((1, n))],
        out_specs=pl.BlockSpec((tm, n), lambda i: (i, 0)),
        out_shape=jax.ShapeDtypeStruct((t, n), F32),
        compiler_params=_params(("parallel",), 32),
        name="ssd_dt_proj",
    )(x2, w, bias_row)


CONV_HALO = 16
CONV_COLS = 2048
CONV_ROWS = 64


def _conv_kernel(prev_ref, cur_ref, next_ref, w_ref, b_ref, out_ref, ext_ref):
    i = pl.program_id(1)
    tc = cur_ref.shape[0]
    has_prev = (i > 0).astype(F32)
    has_next = (i < pl.num_programs(1) - 1).astype(F32)
    for ct in range(CONV_COLS // LANES):
        lanes = slice(ct * LANES, (ct + 1) * LANES)
        ext_ref[ct, 0:CONV_HALO, :] = prev_ref[:, lanes].astype(F32) * has_prev
        ext_ref[ct, CONV_HALO:CONV_HALO + tc, :] = cur_ref[:, lanes].astype(F32)
        ext_ref[ct, CONV_HALO + tc:, :] = next_ref[:, lanes].astype(F32) * has_next
    for ct in range(CONV_COLS // LANES):
        lanes = slice(ct * LANES, (ct + 1) * LANES)
        taps = [w_ref[k:k + 1, lanes] for k in range(D_CONV)]
        bias = b_ref[:, lanes]
        for rb in range(tc // CONV_ROWS):
            base = CONV_HALO - CONV_PAD + rb * CONV_ROWS
            acc = bias + ext_ref[ct, base:base + CONV_ROWS, :] * taps[0]
            for k in range(1, D_CONV):
                acc = acc + ext_ref[ct, base + k:base + k + CONV_ROWS, :] * taps[k]
            out_ref[rb * CONV_ROWS:(rb + 1) * CONV_ROWS, lanes] = _silu(acc).astype(BF16)


def _conv(zx, conv_w, conv_b, batch, seq, tc=512):
    t = zx.shape[0]
    nt = seq // tc
    per = tc // CONV_HALO
    last_halo = t // CONV_HALO - 1
    skip = D_INNER // CONV_COLS
    return pl.pallas_call(
        _conv_kernel,
        grid=(batch, nt, CONV_DIM // CONV_COLS),
        in_specs=[pl.BlockSpec((CONV_HALO, CONV_COLS),
                               lambda b, i, j: (jnp.maximum((b * nt + i) * per - 1, 0), j + skip)),
                  pl.BlockSpec((tc, CONV_COLS), lambda b, i, j: (b * nt + i, j + skip)),
                  pl.BlockSpec((CONV_HALO, CONV_COLS),
                               lambda b, i, j: (jnp.minimum((b * nt + i + 1) * per, last_halo), j + skip)),
                  pl.BlockSpec((D_CONV, CONV_COLS), lambda b, i, j: (0, j)),
                  pl.BlockSpec((1, CONV_COLS), lambda b, i, j: (0, j))],
        out_specs=pl.BlockSpec((tc, CONV_COLS), lambda b, i, j: (b * nt + i, j)),
        out_shape=jax.ShapeDtypeStruct((t, CONV_DIM), BF16),
        scratch_shapes=[pltpu.VMEM((CONV_COLS // LANES, tc + 2 * CONV_HALO, LANES), F32)],
        compiler_params=_params(("parallel", "parallel", "parallel"), 32),
        name="ssd_conv",
    )(zx, zx, zx, conv_w, conv_b)


def _split3(v):
    hi = v.astype(BF16)
    r = v - hi.astype(F32)
    mid = r.astype(BF16)
    lo = (r - mid.astype(F32)).astype(BF16)
    return hi, mid, lo


def _scan_direction(xs_ref, bm_ref, cm_ref, dt_ref, alog, e_ref, y_ref, state_ref, reverse):
    row = lax.broadcasted_iota(jnp.int32, (CHUNK, CHUNK), 0)
    col = lax.broadcasted_iota(jnp.int32, (CHUNK, CHUNK), 1)
    mask = (col >= row) if reverse else (row >= col)
    cum = mask.astype(BF16)
    low_half = col < SSD_HEAD_DIM

    dt = dt_ref[0]
    a = dt * -jnp.exp(alog)
    a_hi, a_mid, a_lo = _split3(a)
    acs = (jnp.dot(cum, a_hi, preferred_element_type=F32) + jnp.dot(cum, a_mid, preferred_element_type=F32)
           + jnp.dot(cum, a_lo, preferred_element_type=F32))
    tot = jnp.sum(a, axis=0, keepdims=True)
    src_t = (acs - jnp.log(dt)).T

    e_in = jnp.exp(acs).astype(BF16)
    e_out = (jnp.exp(tot - acs) * dt).astype(BF16)
    e_tot = jnp.broadcast_to(jnp.exp(tot), (8, SSD_HEADS)).astype(BF16)
    zero = jnp.zeros((CHUNK, LANES), BF16)

    def prep(g):
        gl = slice(g * GROUP_WIDTH, (g + 1) * GROUP_WIDTH)
        bg = bm_ref[:, g * D_STATE:(g + 1) * D_STATE]
        cg = cm_ref[:, g * D_STATE:(g + 1) * D_STATE]
        cb = lax.dot_general(cg, bg, (((1,), (1,)), ((), ())), preferred_element_type=F32)
        y_raw = jnp.dot(cg, state_ref[:, gl].astype(BF16), preferred_element_type=F32)
        e = e_ref[:, gl]
        decay_in = jnp.dot(e_in, e, preferred_element_type=F32)
        decay_out = jnp.dot(e_out, e, preferred_element_type=F32).astype(BF16)
        decay_tot = jnp.dot(e_tot, e, preferred_element_type=F32)[0:1]
        return bg, cb, y_raw * decay_in, decay_out, decay_tot

    ready = prep(0)
    yield
    for g in range(N_GROUPS):
        gl = slice(g * GROUP_WIDTH, (g + 1) * GROUP_WIDTH)
        bg, cb, y_state, decay_out, decay_tot = ready
        if g + 1 < N_GROUPS:
            ready = prep(g + 1)
        ys = []
        for pair in range(HEADS_PER_GROUP // 2):
            ms = []
            for h in (g * HEADS_PER_GROUP + 2 * pair, g * HEADS_PER_GROUP + 2 * pair + 1):
                seg = acs[:, h:h + 1] - src_t[h:h + 1, :]
                ms.append((cb * jnp.exp(jnp.where(mask, seg, -jnp.inf))).astype(BF16))
            xp = xs_ref[:, g * GROUP_WIDTH + pair * LANES:g * GROUP_WIDTH + (pair + 1) * LANES]
            x_diag = jnp.concatenate([jnp.where(low_half, xp, zero), jnp.where(low_half, zero, xp)], axis=0)
            ys.append(jnp.dot(jnp.concatenate(ms, axis=1), x_diag, preferred_element_type=F32))
        y_ref[:, gl] = (jnp.concatenate(ys, axis=1) + y_state).astype(y_ref.dtype)
        xw = xs_ref[:, gl] * decay_out
        bg_t = bg.astype(F32).T.astype(BF16)
        state_ref[:, gl] = state_ref[:, gl] * decay_tot + jnp.dot(bg_t, xw, preferred_element_type=F32)
        yield


def _scan_kernel(xs_f, bm_f, cm_f, dt_f, xs_b, bm_b, cm_b, dt_b, alog_ref, e_ref, y_f, y_b, state_f, state_b):
    @pl.when(pl.program_id(1) == 0)
    def _():
        state_f[...] = jnp.zeros_like(state_f)
        state_b[...] = jnp.zeros_like(state_b)

    streams = [_scan_direction(xs_f, bm_f, cm_f, dt_f, alog_ref[0], e_ref, y_f, state_f, False),
               _scan_direction(xs_b, bm_b, cm_b, dt_b, alog_ref[1], e_ref, y_b, state_b, True)]
    for _ in range(N_GROUPS + 1):
        for stream in streams:
            next(stream)


def _scan(xbc, dt2, alog, batch, seq):
    t = xbc.shape[0]
    nc = seq // CHUNK
    heads = jnp.arange(D_INNER, dtype=jnp.int32) // SSD_HEAD_DIM
    expand = (heads[None, :] == jnp.arange(SSD_HEADS, dtype=jnp.int32)[:, None]).astype(BF16)
    b_block = D_INNER // BC_DIM
    fwd = lambda b, c: b * nc + c
    bwd = lambda b, c: b * nc + nc - 1 - c

    def chunk_specs(chunk, direction):
        return [pl.BlockSpec((CHUNK, D_INNER), lambda b, c: (chunk(b, c), 0)),
                pl.BlockSpec((CHUNK, BC_DIM), lambda b, c: (chunk(b, c), b_block)),
                pl.BlockSpec((CHUNK, BC_DIM), lambda b, c: (chunk(b, c), b_block + 1)),
                pl.BlockSpec((1, CHUNK, SSD_HEADS), lambda b, c: (direction, chunk(b, c), 0))]

    y_shape = jax.ShapeDtypeStruct((t, D_INNER), BF16)
    return pl.pallas_call(
        _scan_kernel,
        grid=(batch, nc),
        in_specs=chunk_specs(fwd, 0) + chunk_specs(bwd, 1)
        + [pl.BlockSpec((2, 1, SSD_HEADS), lambda b, c: (0, 0, 0)),
           pl.BlockSpec((SSD_HEADS, D_INNER), lambda b, c: (0, 0))],
        out_specs=[pl.BlockSpec((CHUNK, D_INNER), lambda b, c: (fwd(b, c), 0)),
                   pl.BlockSpec((CHUNK, D_INNER), lambda b, c: (bwd(b, c), 0))],
        out_shape=[y_shape, y_shape],
        scratch_shapes=[pltpu.VMEM((D_STATE, D_INNER), F32), pltpu.VMEM((D_STATE, D_INNER), F32)],
        compiler_params=_params(("parallel", "arbitrary"), 48),
        name="ssd_scan",
    )(xbc, xbc, xbc, dt2, xbc, xbc, xbc, dt2, alog, expand)


def _ssd_out_kernel(yf_ref, yb_ref, xs_ref, z_ref, x_ref, p_ref, dskip_ref, nw_ref, wout_ref,
                    lng_ref, lnb_ref, wpe_ref, wpg_ref, out_ref):
    tm = x_ref.shape[0]
    sum_sq = jnp.zeros((tm, 1), F32)
    h = jnp.zeros((tm, D_MODEL), F32)
    for kc in range(D_INNER // OUT_CHUNK):
        cols = slice(kc * OUT_CHUNK, (kc + 1) * OUT_CHUNK)
        y = (yf_ref[:, cols].astype(F32) + yb_ref[:, cols].astype(F32)
             + xs_ref[:, cols].astype(F32) * dskip_ref[:, cols])
        u = y * _silu(z_ref[:, cols].astype(F32))
        sum_sq = sum_sq + jnp.sum(u * u, axis=-1, keepdims=True)
        h = h + jnp.dot((u * nw_ref[:, cols]).astype(BF16), wout_ref[cols, :], preferred_element_type=F32)
    h = h * lax.rsqrt(sum_sq * (1.0 / D_INNER) + RMS_EPS)
    out_ref[...] = _deepnorm_ple(x_ref[...], h, p_ref[...], lng_ref[...], lnb_ref[...], wpe_ref[...], wpg_ref[...])


OUT_CHUNK = 1024


def _ssd_out(y_f, y_b, xbc, zx, x2, p2, dskip, nw, wout, lng, lnb, wpe, wpg, tm=256):
    t = x2.shape[0]
    row = lambda w: pl.BlockSpec((tm, w), lambda i: (i, 0))
    return pl.pallas_call(
        _ssd_out_kernel,
        grid=(t // tm,),
        in_specs=[row(D_INNER), row(D_INNER), row(D_INNER), row(D_INNER), row(D_MODEL), _ple_rows(1, tm),
                  _resident((1, D_INNER)), _resident((1, D_INNER)), _resident((D_INNER, D_MODEL)),
                  _resident((1, D_MODEL)), _resident((1, D_MODEL)),
                  _resident((PLE_DIM, D_MODEL)), _resident((D_MODEL, D_MODEL))],
        out_specs=row(D_MODEL),
        out_shape=jax.ShapeDtypeStruct((t, D_MODEL), F32),
        compiler_params=_params(("parallel",), 56),
        name="ssd_out",
    )(y_f, y_b, xbc, zx, x2, p2, dskip, nw, wout, lng, lnb, wpe, wpg)


def kernel(x, p, positions, ln_g, ln_b, ple_w_proj, ple_w_gate, mla_w_in, mla_q_norm, mla_kv_norm, mla_w_uq,
           mla_w_ukv, mla_w_o, ssd_w_in, ssd_conv_w, ssd_conv_b, ssd_dt_bias, ssd_a_log, ssd_d, ssd_norm, ssd_w_out):
    batch, seq, _ = x.shape
    t = batch * seq
    x2 = x.reshape(t, D_MODEL)
    p2 = p.reshape(DEPTH, t, PLE_DIM)
    row = lambda v: v.reshape(1, -1)

    inv_freq = 1.0 / (ROPE_BASE ** (jnp.arange(0, QK_ROPE, 2, dtype=F32) / QK_ROPE))
    ct, sa, sb, cos_t, sin_t = _rope_tables(positions.astype(F32).reshape(1, t), inv_freq.reshape(QK_ROPE // 2, 1))

    lat = Q_LORA + KV_LORA
    w_in = mla_w_in[0]
    w_lat = jnp.pad(w_in[:, :lat + QK_ROPE].astype(BF16), ((0, 0), (0, ROPE_PAD - QK_ROPE)))
    w_z = w_in[:, lat + QK_ROPE:].astype(BF16)
    wuqt_p = mla_w_uq[0].astype(BF16).reshape(Q_LORA, MLA_HEADS, QK_HEAD).transpose(1, 2, 0)
    wukv_p = mla_w_ukv[0].reshape(KV_LORA, MLA_HEADS, QK_NOPE + V_HEAD).transpose(1, 0, 2).astype(BF16)

    cqt, ckvn, kr, gate = _mla_in(x2, w_lat, w_z, row(mla_q_norm[0]), row(mla_kv_norm[0]), ct, sa, sb)
    qt, k, vt = _mla_up(cqt, ckvn, kr, cos_t, sin_t, wuqt_p, wukv_p, batch, seq)
    o = _attention(qt, k, vt).reshape(t, MLA_WIDTH)
    x2 = _mla_out(o, gate, x2, p2, mla_w_o[0].astype(BF16), row(ln_g[0]), row(ln_b[0]),
                  ple_w_proj[0].astype(BF16), ple_w_gate[0].astype(BF16))

    w_ssd = ssd_w_in[0].astype(BF16)
    zx = _proj(x2, w_ssd, D_INNER + CONV_DIM, BF16)
    dt = _dt_proj(x2, w_ssd, (D_INNER + CONV_DIM) // (2 * SSD_HEADS), ssd_dt_bias[0].reshape(1, 2 * SSD_HEADS))
    xbc = _conv(zx, ssd_conv_w[0], row(ssd_conv_b[0]), batch, seq)
    dt2 = dt.reshape(t, 2, SSD_HEADS).transpose(1, 0, 2)
    y_f, y_b = _scan(xbc, dt2, ssd_a_log[0].reshape(2, 1, SSD_HEADS), batch, seq)
    dskip = jnp.repeat(ssd_d[0], SSD_HEAD_DIM).reshape(1, D_INNER)
    x2 = _ssd_out(y_f, y_b, xbc, zx, x2, p2, dskip, row(ssd_norm[0]), ssd_w_out[0].astype(BF16),
                  row(ln_g[1]), row(ln_b[1]), ple_w_proj[1].astype(BF16), ple_w_gate[1].astype(BF16))
    return x2.reshape(batch, seq, D_MODEL)
```

```python
import functools

import jax
import jax.numpy as jnp
from jax import lax
from jax.experimental import pallas as pl
from jax.experimental.pallas import tpu as pltpu

F32 = jnp.float32
BF16 = jnp.bfloat16

D_MODEL = 2048
DEPTH = 2
ALPHA = (2 * DEPTH) ** 0.25
PLE_DIM = 256

MLA_HEADS = 16
Q_LORA = 512
KV_LORA = 512
QK_NOPE = 128
QK_ROPE = 64
V_HEAD = 128
QK_HEAD = QK_NOPE + QK_ROPE
MLA_WIDTH = MLA_HEADS * V_HEAD
ROPE_BASE = 10000.0
ATTN_SCALE = QK_HEAD ** -0.5
LOG2_E = 1.4426950408889634
QK_PAD = QK_HEAD
ROPE_PAD = 128

D_INNER = 4096
SSD_HEAD_DIM = 64
SSD_HEADS = 64
N_GROUPS = 8
HEADS_PER_GROUP = 8
D_STATE = 128
D_CONV = 7
CONV_PAD = D_CONV // 2
GROUP_WIDTH = HEADS_PER_GROUP * SSD_HEAD_DIM
BC_DIM = N_GROUPS * D_STATE
CONV_DIM = D_INNER + 2 * BC_DIM
CHUNK = 128
LANES = 128

LN_EPS = 1e-5
RMS_EPS = 1e-6

MIB = 1024 * 1024


def _params(semantics, vmem_mib):
    return pltpu.CompilerParams(dimension_semantics=semantics, vmem_limit_bytes=vmem_mib * MIB)


def _resident(shape, index=None):
    index = (0,) * len(shape) if index is None else index
    return pl.BlockSpec(shape, lambda *_: index, pipeline_mode=pl.Buffered(1))


def _rmsnorm(v, g):
    return v * lax.rsqrt(jnp.mean(v * v, axis=-1, keepdims=True) + RMS_EPS) * g


def _layernorm(v, g, b):
    mu = jnp.mean(v, axis=-1, keepdims=True)
    d = v - mu
    var = jnp.mean(d * d, axis=-1, keepdims=True)
    return d * lax.rsqrt(var + LN_EPS) * g + b


def _silu(v):
    return v * jax.nn.sigmoid(v)


def _rope(u, ct, sa, sb):
    return u * ct + pltpu.roll(u, QK_ROPE // 2, 1) * sa + pltpu.roll(u, ROPE_PAD - QK_ROPE // 2, 1) * sb


def _rope_table_kernel(pos_ref, freq_ref, ct_ref, sa_ref, sb_ref, cos_t_ref, sin_t_ref):
    ang = pos_ref[...] * freq_ref[...]
    c = jnp.cos(ang)
    s = jnp.sin(ang)
    cos_t_ref[...] = c
    sin_t_ref[...] = s
    zero = jnp.zeros_like(c)
    pad = jnp.zeros((ROPE_PAD - QK_ROPE, c.shape[1]), F32)
    ct_ref[...] = jnp.concatenate([c, c, pad], axis=0).T
    sa_ref[...] = jnp.concatenate([zero, s, pad], axis=0).T
    sb_ref[...] = jnp.concatenate([-s, zero, pad], axis=0).T


def _rope_tables(pos_row, freq_col, tm=2048):
    t = pos_row.shape[1]
    tab = jax.ShapeDtypeStruct((t, ROPE_PAD), F32)
    tab_t = jax.ShapeDtypeStruct((QK_ROPE // 2, t), F32)
    row = pl.BlockSpec((tm, ROPE_PAD), lambda i: (i, 0))
    col = pl.BlockSpec((QK_ROPE // 2, tm), lambda i: (0, i))
    return pl.pallas_call(
        _rope_table_kernel,
        grid=(t // tm,),
        in_specs=[pl.BlockSpec((1, tm), lambda i: (0, i)), pl.BlockSpec((QK_ROPE // 2, 1), lambda i: (0, 0))],
        out_specs=[row, row, row, col, col],
        out_shape=[tab, tab, tab, tab_t, tab_t],
        compiler_params=_params(("parallel",), 32),
        name="rope_tables",
    )(pos_row, freq_col)


def _mla_in_kernel(x_ref, wlat_ref, wz_ref, qn_ref, kvn_ref, ct_ref, sa_ref, sb_ref, cqt_ref, ckv_ref, kr_ref,
                   g_ref):
    xb = x_ref[...].astype(BF16)
    h = jnp.dot(xb, wlat_ref[...], preferred_element_type=F32)
    cqt_ref[...] = _rmsnorm(h[:, :Q_LORA], qn_ref[...]).T.astype(BF16)
    ckv_ref[...] = _rmsnorm(h[:, Q_LORA:Q_LORA + KV_LORA], kvn_ref[...]).astype(BF16)
    kr_ref[...] = _rope(h[:, Q_LORA + KV_LORA:], ct_ref[...], sa_ref[...], sb_ref[...]).astype(BF16)
    z = jnp.dot(xb, wz_ref[...], preferred_element_type=F32)
    g_ref[...] = _silu(z).astype(BF16)


def _mla_in(x2, w_lat, w_z, qn, kvn, ct, sa, sb, tm=512):
    t = x2.shape[0]
    row = lambda w: pl.BlockSpec((tm, w), lambda i: (i, 0))
    return pl.pallas_call(
        _mla_in_kernel,
        grid=(t // tm,),
        in_specs=[row(D_MODEL), _resident(w_lat.shape), _resident(w_z.shape), _resident((1, Q_LORA)),
                  _resident((1, KV_LORA)),
                  row(ROPE_PAD), row(ROPE_PAD), row(ROPE_PAD)],
        out_specs=[pl.BlockSpec((Q_LORA, tm), lambda i: (0, i)), row(KV_LORA), row(ROPE_PAD), row(MLA_WIDTH)],
        out_shape=[jax.ShapeDtypeStruct((Q_LORA, t), BF16), jax.ShapeDtypeStruct((t, KV_LORA), BF16),
                   jax.ShapeDtypeStruct((t, ROPE_PAD), BF16), jax.ShapeDtypeStruct((t, MLA_WIDTH), BF16)],
        compiler_params=_params(("parallel",), 48),
        name="mla_in",
    )(x2, w_lat, w_z, qn, kvn, ct, sa, sb)


UP_HEADS = 8


def _mla_up_kernel(cqt_ref, ckv_ref, kr_ref, cos_ref, sin_ref, wuqt_ref, wukv_ref, qt_ref, k_ref, vt_ref):
    half = QK_ROPE // 2
    c = cos_ref[...]
    s = sin_ref[...]
    for j in range(UP_HEADS):
        qt = jnp.dot(wuqt_ref[j], cqt_ref[...], preferred_element_type=F32) * (ATTN_SCALE * LOG2_E)
        t1 = qt[QK_NOPE:QK_NOPE + half]
        t2 = qt[QK_NOPE + half:QK_HEAD]
        qt_ref[0, j, :QK_NOPE, :] = qt[:QK_NOPE].astype(BF16)
        qt_ref[0, j, QK_NOPE:QK_NOPE + half, :] = (t1 * c - t2 * s).astype(BF16)
        qt_ref[0, j, QK_NOPE + half:QK_HEAD, :] = (t2 * c + t1 * s).astype(BF16)
        kv = jnp.dot(ckv_ref[...], wukv_ref[j], preferred_element_type=F32)
        k_ref[0, j, :, :QK_NOPE] = kv[:, :QK_NOPE].astype(BF16)
        k_ref[0, j, :, QK_NOPE:] = kr_ref[:, :QK_ROPE]
        vt_ref[0, j] = kv[:, QK_NOPE:].T.astype(BF16)


def _mla_up(cqt, ckvn, kr, cos_t, sin_t, wuqt_p, wukv_p, batch, seq, tm=1024):
    nt = seq // tm
    row = lambda w: pl.BlockSpec((tm, w), lambda b, i, h: (b * nt + i, 0))
    col = lambda w: pl.BlockSpec((w, tm), lambda b, i, h: (0, b * nt + i))
    feat = lambda w: pl.BlockSpec((1, UP_HEADS, w, tm), lambda b, i, h: (b, h, 0, i))
    return pl.pallas_call(
        _mla_up_kernel,
        grid=(batch, nt, MLA_HEADS // UP_HEADS),
        in_specs=[col(Q_LORA), row(KV_LORA), row(ROPE_PAD), col(QK_ROPE // 2), col(QK_ROPE // 2),
                  pl.BlockSpec((UP_HEADS, QK_PAD, Q_LORA), lambda b, i, h: (h, 0, 0)),
                  pl.BlockSpec((UP_HEADS, KV_LORA, QK_NOPE + V_HEAD), lambda b, i, h: (h, 0, 0))],
        out_specs=[feat(QK_PAD), pl.BlockSpec((1, UP_HEADS, tm, QK_PAD), lambda b, i, h: (b, h, i, 0)),
                   feat(V_HEAD)],
        out_shape=[jax.ShapeDtypeStruct((batch, MLA_HEADS, QK_PAD, seq), BF16),
                   jax.ShapeDtypeStruct((batch, MLA_HEADS, seq, QK_PAD), BF16),
                   jax.ShapeDtypeStruct((batch, MLA_HEADS, V_HEAD, seq), BF16)],
        compiler_params=_params(("parallel", "parallel", "arbitrary"), 40),
        name="mla_up",
    )(cqt, ckvn, kr, cos_t, sin_t, wuqt_p, wukv_p)


def _attn_kernel(qt_ref, k_ref, vt_ref, o_ref, s_ref, *, tk, n_col):
    tq = qt_ref.shape[3]
    w = tq // n_col
    n_chunks = k_ref.shape[2] // tk

    def scores(c, slot, g):
        s = jnp.dot(k_ref[0, 0, pl.ds(c * tk, tk), :], qt_ref[0, 0, :, g * w:(g + 1) * w],
                    preferred_element_type=F32)
        s_ref[slot, :, g * w:(g + 1) * w] = s
        return jnp.max(s, axis=0, keepdims=True)

    def consume(c, slot, g, m, m_chunk, l, acc):
        m_new = jnp.maximum(m, m_chunk)
        alpha = jnp.exp2(m - m_new)
        p = jnp.exp2(s_ref[slot, :, g * w:(g + 1) * w] - m_new)
        l = alpha * l + jnp.sum(p.reshape(tk // 8, 8, w), axis=0)
        pv = jnp.dot(vt_ref[0, 0, :, pl.ds(c * tk, tk)], p.astype(BF16), preferred_element_type=F32)
        return m_new, l, alpha * acc + pv

    carry = [(jnp.full((1, w), -jnp.inf, F32), scores(0, 0, g), jnp.zeros((8, w), F32),
              jnp.zeros((V_HEAD, w), F32)) for g in range(n_col)]
    for c in range(n_chunks):
        slot = c % 2
        for g in range(n_col):
            m, m_chunk, l, acc = carry[g]
            m_next = scores(c + 1, 1 - slot, g) if c + 1 < n_chunks else m_chunk
            m, l, acc = consume(c, slot, g, m, m_chunk, l, acc)
            carry[g] = (m, m_next, l, acc)
    for g in range(n_col):
        _, _, l, acc = carry[g]
        o_ref[0, g * w:(g + 1) * w, :] = (acc / jnp.sum(l, axis=0, keepdims=True)).T.astype(o_ref.dtype)


def _attention(qt, k, vt, tq=1024, tk=512, n_col=2):
    batch, heads, seq, _ = k.shape
    assert seq % (2 * tk) == 0 and seq % tq == 0
    return pl.pallas_call(
        functools.partial(_attn_kernel, tk=tk, n_col=n_col),
        grid=(batch, heads, seq // tq),
        in_specs=[pl.BlockSpec((1, 1, QK_PAD, tq), lambda b, h, i: (b, h, 0, i)),
                  pl.BlockSpec((1, 1, seq, QK_PAD), lambda b, h, i: (b, h, 0, 0)),
                  pl.BlockSpec((1, 1, V_HEAD, seq), lambda b, h, i: (b, h, 0, 0))],
        out_specs=pl.BlockSpec((1, tq, V_HEAD), lambda b, h, i: (b, i, h)),
        out_shape=jax.ShapeDtypeStruct((batch, seq, heads * V_HEAD), BF16),
        scratch_shapes=[pltpu.VMEM((2, tk, tq), F32)],
        compiler_params=_params(("parallel", "parallel", "arbitrary"), 48),
        name="mla_attention",
    )(qt, k, vt)


def _ple_rows(layer, tm):
    return pl.BlockSpec((None, tm, PLE_DIM), lambda i: (layer, i, 0))


def _deepnorm_ple(x, h, p, lng, lnb, wpe, wpg):
    y = _layernorm(ALPHA * x + h, lng, lnb)
    emb = jnp.dot(p.astype(BF16), wpe, preferred_element_type=F32)
    gate = jax.nn.sigmoid(jnp.dot(y.astype(BF16), wpg, preferred_element_type=F32))
    return y + emb * gate


def _mla_out_kernel(o_ref, g_ref, x_ref, p_ref, wo_ref, lng_ref, lnb_ref, wpe_ref, wpg_ref, out_ref):
    og = (o_ref[...].astype(F32) * g_ref[...].astype(F32)).astype(BF16)
    h = jnp.dot(og, wo_ref[...], preferred_element_type=F32)
    out_ref[...] = _deepnorm_ple(x_ref[...], h, p_ref[...], lng_ref[...], lnb_ref[...], wpe_ref[...], wpg_ref[...])


def _mla_out(o, g, x2, p2, wo, lng, lnb, wpe, wpg, tm=512):
    t = x2.shape[0]
    row = lambda w: pl.BlockSpec((tm, w), lambda i: (i, 0))
    return pl.pallas_call(
        _mla_out_kernel,
        grid=(t // tm,),
        in_specs=[row(MLA_WIDTH), row(MLA_WIDTH), row(D_MODEL), _ple_rows(0, tm),
                  _resident((MLA_WIDTH, D_MODEL)), _resident((1, D_MODEL)), _resident((1, D_MODEL)),
                  _resident((PLE_DIM, D_MODEL)), _resident((D_MODEL, D_MODEL))],
        out_specs=row(D_MODEL),
        out_shape=jax.ShapeDtypeStruct((t, D_MODEL), F32),
        compiler_params=_params(("parallel",), 56),
        name="mla_out",
    )(o, g, x2, p2, wo, lng, lnb, wpe, wpg)


def _proj_kernel(x_ref, w_ref, o_ref, xb_ref):
    @pl.when(pl.program_id(1) == 0)
    def _():
        xb_ref[...] = x_ref[...].astype(BF16)

    o_ref[...] = jnp.dot(xb_ref[...], w_ref[...], preferred_element_type=F32).astype(o_ref.dtype)


def _proj(x2, w, n, out_dtype, tm=1024, tn=2048):
    t, kdim = x2.shape
    return pl.pallas_call(
        _proj_kernel,
        grid=(t // tm, n // tn),
        in_specs=[pl.BlockSpec((tm, kdim), lambda i, j: (i, 0)), pl.BlockSpec((kdim, tn), lambda i, j: (0, j))],
        out_specs=pl.BlockSpec((tm, tn), lambda i, j: (i, j)),
        out_shape=jax.ShapeDtypeStruct((t, n), out_dtype),
        scratch_shapes=[pltpu.VMEM((tm, kdim), BF16)],
        compiler_params=_params(("parallel", "arbitrary"), 56),
        name="ssd_in_proj",
    )(x2, w)


def _dt_kernel(x_ref, w_ref, bias_ref, o_ref):
    h = jnp.dot(x_ref[...].astype(BF16), w_ref[...], preferred_element_type=F32) + bias_ref[...]
    o_ref[...] = jnp.maximum(h, 0.0) + jnp.log1p(jnp.exp(-jnp.abs(h)))


def _dt_proj(x2, w, col_block, bias_row, tm=1024):
    t, kdim = x2.shape
    n = bias_row.shape[1]
    return pl.pallas_call(
        _dt_kernel,
        grid=(t // tm,),
        in_specs=[pl.BlockSpec((tm, kdim), lambda i: (i, 0)), _resident((kdim, n), (0, col_block)), _resident((1, n))],
        out_specs=pl.BlockSpec((tm, n), lambda i: (i, 0)),
        out_shape=jax.ShapeDtypeStruct((t, n), F32),
        compiler_params=_params(("parallel",), 32),
        name="ssd_dt_proj",
    )(x2, w, bias_row)


CONV_HALO = 16
CONV_COLS = 2048
CONV_ROWS = 64


def _conv_kernel(prev_ref, cur_ref, next_ref, w_ref, b_ref, out_ref, ext_ref):
    i = pl.program_id(1)
    tc = cur_ref.shape[0]
    has_prev = (i > 0).astype(F32)
    has_next = (i < pl.num_programs(1) - 1).astype(F32)
    for ct in range(CONV_COLS // LANES):
        lanes = slice(ct * LANES, (ct + 1) * LANES)
        ext_ref[ct, 0:CONV_HALO, :] = prev_ref[:, lanes].astype(F32) * has_prev
        ext_ref[ct, CONV_HALO:CONV_HALO + tc, :] = cur_ref[:, lanes].astype(F32)
        ext_ref[ct, CONV_HALO + tc:, :] = next_ref[:, lanes].astype(F32) * has_next
    for ct in range(CONV_COLS // LANES):
        lanes = slice(ct * LANES, (ct + 1) * LANES)
        taps = [w_ref[k:k + 1, lanes] for k in range(D_CONV)]
        bias = b_ref[:, lanes]
        for rb in range(tc // CONV_ROWS):
            base = CONV_HALO - CONV_PAD + rb * CONV_ROWS
            acc = bias + ext_ref[ct, base:base + CONV_ROWS, :] * taps[0]
            for k in range(1, D_CONV):
                acc = acc + ext_ref[ct, base + k:base + k + CONV_ROWS, :] * taps[k]
            out_ref[rb * CONV_ROWS:(rb + 1) * CONV_ROWS, lanes] = _silu(acc).astype(BF16)


def _conv(zx, conv_w, conv_b, batch, seq, tc=512):
    t = zx.shape[0]
    nt = seq // tc
    per = tc // CONV_HALO
    last_halo = t // CONV_HALO - 1
    skip = D_INNER // CONV_COLS
    return pl.pallas_call(
        _conv_kernel,
        grid=(batch, nt, CONV_DIM // CONV_COLS),
        in_specs=[pl.BlockSpec((CONV_HALO, CONV_COLS),
                               lambda b, i, j: (jnp.maximum((b * nt + i) * per - 1, 0), j + skip)),
                  pl.BlockSpec((tc, CONV_COLS), lambda b, i, j: (b * nt + i, j + skip)),
                  pl.BlockSpec((CONV_HALO, CONV_COLS),
                               lambda b, i, j: (jnp.minimum((b * nt + i + 1) * per, last_halo), j + skip)),
                  pl.BlockSpec((D_CONV, CONV_COLS), lambda b, i, j: (0, j)),
                  pl.BlockSpec((1, CONV_COLS), lambda b, i, j: (0, j))],
        out_specs=pl.BlockSpec((tc, CONV_COLS), lambda b, i, j: (b * nt + i, j)),
        out_shape=jax.ShapeDtypeStruct((t, CONV_DIM), BF16),
        scratch_shapes=[pltpu.VMEM((CONV_COLS // LANES, tc + 2 * CONV_HALO, LANES), F32)],
        compiler_params=_params(("parallel", "parallel", "parallel"), 32),
        name="ssd_conv",
    )(zx, zx, zx, conv_w, conv_b)


def _split3(v):
    hi = v.astype(BF16)
    r = v - hi.astype(F32)
    mid = r.astype(BF16)
    lo = (r - mid.astype(F32)).astype(BF16)
    return hi, mid, lo


def _scan_direction(xs_ref, bm_ref, cm_ref, dt_ref, alog, e_ref, y_ref, state_ref, reverse):
    row = lax.broadcasted_iota(jnp.int32, (CHUNK, CHUNK), 0)
    col = lax.broadcasted_iota(jnp.int32, (CHUNK, CHUNK), 1)
    mask = (col >= row) if reverse else (row >= col)
    cum = mask.astype(BF16)
    low_half = col < SSD_HEAD_DIM

    dt = dt_ref[0]
    a = dt * -jnp.exp(alog)
    a_hi, a_mid, a_lo = _split3(a)
    acs = (jnp.dot(cum, a_hi, preferred_element_type=F32) + jnp.dot(cum, a_mid, preferred_element_type=F32)
           + jnp.dot(cum, a_lo, preferred_element_type=F32))
    tot = jnp.sum(a, axis=0, keepdims=True)
    src_t = (acs - jnp.log(dt)).T

    e_in = jnp.exp(acs).astype(BF16)
    e_out = (jnp.exp(tot - acs) * dt).astype(BF16)
    e_tot = jnp.broadcast_to(jnp.exp(tot), (8, SSD_HEADS)).astype(BF16)
    zero = jnp.zeros((CHUNK, LANES), BF16)

    def prep(g):
        gl = slice(g * GROUP_WIDTH, (g + 1) * GROUP_WIDTH)
        bg = bm_ref[:, g * D_STATE:(g + 1) * D_STATE]
        cg = cm_ref[:, g * D_STATE:(g + 1) * D_STATE]
        cb = lax.dot_general(cg, bg, (((1,), (1,)), ((), ())), preferred_element_type=F32)
        y_raw = jnp.dot(cg, state_ref[:, gl].astype(BF16), preferred_element_type=F32)
        e = e_ref[:, gl]
        decay_in = jnp.dot(e_in, e, preferred_element_type=F32)
        decay_out = jnp.dot(e_out, e, preferred_element_type=F32).astype(BF16)
        decay_tot = jnp.dot(e_tot, e, preferred_element_type=F32)[0:1]
        return bg, cb, y_raw * decay_in, decay_out, decay_tot

    ready = prep(0)
    yield
    for g in range(N_GROUPS):
        gl = slice(g * GROUP_WIDTH, (g + 1) * GROUP_WIDTH)
        bg, cb, y_state, decay_out, decay_tot = ready
        if g + 1 < N_GROUPS:
            ready = prep(g + 1)
        ys = []
        for pair in range(HEADS_PER_GROUP // 2):
            ms = []
            for h in (g * HEADS_PER_GROUP + 2 * pair, g * HEADS_PER_GROUP + 2 * pair + 1):
                seg = acs[:, h:h + 1] - src_t[h:h + 1, :]
                ms.append((cb * jnp.exp(jnp.where(mask, seg, -jnp.inf))).astype(BF16))
            xp = xs_ref[:, g * GROUP_WIDTH + pair * LANES:g * GROUP_WIDTH + (pair + 1) * LANES]
            x_diag = jnp.concatenate([jnp.where(low_half, xp, zero), jnp.where(low_half, zero, xp)], axis=0)
            ys.append(jnp.dot(jnp.concatenate(ms, axis=1), x_diag, preferred_element_type=F32))
        y_ref[:, gl] = (jnp.concatenate(ys, axis=1) + y_state).astype(y_ref.dtype)
        xw = xs_ref[:, gl] * decay_out
        bg_t = bg.astype(F32).T.astype(BF16)
        state_ref[:, gl] = state_ref[:, gl] * decay_tot + jnp.dot(bg_t, xw, preferred_element_type=F32)
        yield


def _scan_kernel(xs_f, bm_f, cm_f, dt_f, xs_b, bm_b, cm_b, dt_b, alog_ref, e_ref, y_f, y_b, state_f, state_b):
    @pl.when(pl.program_id(1) == 0)
    def _():
        state_f[...] = jnp.zeros_like(state_f)
        state_b[...] = jnp.zeros_like(state_b)

    streams = [_scan_direction(xs_f, bm_f, cm_f, dt_f, alog_ref[0], e_ref, y_f, state_f, False),
               _scan_direction(xs_b, bm_b, cm_b, dt_b, alog_ref[1], e_ref, y_b, state_b, True)]
    for _ in range(N_GROUPS + 1):
        for stream in streams:
            next(stream)


def _scan(xbc, dt2, alog, batch, seq):
    t = xbc.shape[0]
    nc = seq // CHUNK
    heads = jnp.arange(D_INNER, dtype=jnp.int32) // SSD_HEAD_DIM
    expand = (heads[None, :] == jnp.arange(SSD_HEADS, dtype=jnp.int32)[:, None]).astype(BF16)
    b_block = D_INNER // BC_DIM
    fwd = lambda b, c: b * nc + c
    bwd = lambda b, c: b * nc + nc - 1 - c

    def chunk_specs(chunk, direction):
        return [pl.BlockSpec((CHUNK, D_INNER), lambda b, c: (chunk(b, c), 0)),
                pl.BlockSpec((CHUNK, BC_DIM), lambda b, c: (chunk(b, c), b_block)),
                pl.BlockSpec((CHUNK, BC_DIM), lambda b, c: (chunk(b, c), b_block + 1)),
                pl.BlockSpec((1, CHUNK, SSD_HEADS), lambda b, c: (direction, chunk(b, c), 0))]

    y_shape = jax.ShapeDtypeStruct((t, D_INNER), BF16)
    return pl.pallas_call(
        _scan_kernel,
        grid=(batch, nc),
        in_specs=chunk_specs(fwd, 0) + chunk_specs(bwd, 1)
        + [pl.BlockSpec((2, 1, SSD_HEADS), lambda b, c: (0, 0, 0)),
           pl.BlockSpec((SSD_HEADS, D_INNER), lambda b, c: (0, 0))],
        out_specs=[pl.BlockSpec((CHUNK, D_INNER), lambda b, c: (fwd(b, c), 0)),
                   pl.BlockSpec((CHUNK, D_INNER), lambda b, c: (bwd(b, c), 0))],
        out_shape=[y_shape, y_shape],
        scratch_shapes=[pltpu.VMEM((D_STATE, D_INNER), F32), pltpu.VMEM((D_STATE, D_INNER), F32)],
        compiler_params=_params(("parallel", "arbitrary"), 48),
        name="ssd_scan",
    )(xbc, xbc, xbc, dt2, xbc, xbc, xbc, dt2, alog, expand)


def _ssd_out_kernel(yf_ref, yb_ref, xs_ref, z_ref, x_ref, p_ref, dskip_ref, nw_ref, wout_ref,
                    lng_ref, lnb_ref, wpe_ref, wpg_ref, out_ref):
    tm = x_ref.shape[0]
    sum_sq = jnp.zeros((tm, 1), F32)
    h = jnp.zeros((tm, D_MODEL), F32)
    for kc in range(D_INNER // OUT_CHUNK):
        cols = slice(kc * OUT_CHUNK, (kc + 1) * OUT_CHUNK)
        y = (yf_ref[:, cols].astype(F32) + yb_ref[:, cols].astype(F32)
             + xs_ref[:, cols].astype(F32) * dskip_ref[:, cols])
        u = y * _silu(z_ref[:, cols].astype(F32))
        sum_sq = sum_sq + jnp.sum(u * u, axis=-1, keepdims=True)
        h = h + jnp.dot((u * nw_ref[:, cols]).astype(BF16), wout_ref[cols, :], preferred_element_type=F32)
    h = h * lax.rsqrt(sum_sq * (1.0 / D_INNER) + RMS_EPS)
    out_ref[...] = _deepnorm_ple(x_ref[...], h, p_ref[...], lng_ref[...], lnb_ref[...], wpe_ref[...], wpg_ref[...])


OUT_CHUNK = 256


def _ssd_out(y_f, y_b, xbc, zx, x2, p2, dskip, nw, wout, lng, lnb, wpe, wpg, tm=256):
    t = x2.shape[0]
    row = lambda w: pl.BlockSpec((tm, w), lambda i: (i, 0))
    return pl.pallas_call(
        _ssd_out_kernel,
        grid=(t // tm,),
        in_specs=[row(D_INNER), row(D_INNER), row(D_INNER), row(D_INNER), row(D_MODEL), _ple_rows(1, tm),
                  _resident((1, D_INNER)), _resident((1, D_INNER)), _resident((D_INNER, D_MODEL)),
                  _resident((1, D_MODEL)), _resident((1, D_MODEL)),
                  _resident((PLE_DIM, D_MODEL)), _resident((D_MODEL, D_MODEL))],
        out_specs=row(D_MODEL),
        out_shape=jax.ShapeDtypeStruct((t, D_MODEL), F32),
        compiler_params=_params(("parallel",), 56),
        name="ssd_out",
    )(y_f, y_b, xbc, zx, x2, p2, dskip, nw, wout, lng, lnb, wpe, wpg)


def kernel(x, p, positions, ln_g, ln_b, ple_w_proj, ple_w_gate, mla_w_in, mla_q_norm, mla_kv_norm, mla_w_uq,
           mla_w_ukv, mla_w_o, ssd_w_in, ssd_conv_w, ssd_conv_b, ssd_dt_bias, ssd_a_log, ssd_d, ssd_norm, ssd_w_out):
    batch, seq, _ = x.shape
    t = batch * seq
    x2 = x.reshape(t, D_MODEL)
    p2 = p.reshape(DEPTH, t, PLE_DIM)
    row = lambda v: v.reshape(1, -1)

    inv_freq = 1.0 / (ROPE_BASE ** (jnp.arange(0, QK_ROPE, 2, dtype=F32) / QK_ROPE))
    ct, sa, sb, cos_t, sin_t = _rope_tables(positions.astype(F32).reshape(1, t), inv_freq.reshape(QK_ROPE // 2, 1))

    lat = Q_LORA + KV_LORA
    w_in = mla_w_in[0]
    w_lat = jnp.pad(w_in[:, :lat + QK_ROPE].astype(BF16), ((0, 0), (0, ROPE_PAD - QK_ROPE)))
    w_z = w_in[:, lat + QK_ROPE:].astype(BF16)
    wuqt_p = mla_w_uq[0].astype(BF16).reshape(Q_LORA, MLA_HEADS, QK_HEAD).transpose(1, 2, 0)
    wukv_p = mla_w_ukv[0].reshape(KV_LORA, MLA_HEADS, QK_NOPE + V_HEAD).transpose(1, 0, 2).astype(BF16)

    cqt, ckvn, kr, gate = _mla_in(x2, w_lat, w_z, row(mla_q_norm[0]), row(mla_kv_norm[0]), ct, sa, sb)
    qt, k, vt = _mla_up(cqt, ckvn, kr, cos_t, sin_t, wuqt_p, wukv_p, batch, seq)
    o = _attention(qt, k, vt).reshape(t, MLA_WIDTH)
    x2 = _mla_out(o, gate, x2, p2, mla_w_o[0].astype(BF16), row(ln_g[0]), row(ln_b[0]),
                  ple_w_proj[0].astype(BF16), ple_w_gate[0].astype(BF16))

    w_ssd = ssd_w_in[0].astype(BF16)
    zx = _proj(x2, w_ssd, D_INNER + CONV_DIM, BF16)
    dt = _dt_proj(x2, w_ssd, (D_INNER + CONV_DIM) // (2 * SSD_HEADS), ssd_dt_bias[0].reshape(1, 2 * SSD_HEADS))
    xbc = _conv(zx, ssd_conv_w[0], row(ssd_conv_b[0]), batch, seq)
    dt2 = dt.reshape(t, 2, SSD_HEADS).transpose(1, 0, 2)
    y_f, y_b = _scan(xbc, dt2, ssd_a_log[0].reshape(2, 1, SSD_HEADS), batch, seq)
    dskip = jnp.repeat(ssd_d[0], SSD_HEAD_DIM).reshape(1, D_INNER)
    x2 = _ssd_out(y_f, y_b, xbc, zx, x2, p2, dskip, row(ssd_norm[0]), ssd_w_out[0].astype(BF16),
                  row(ln_g[1]), row(ln_b[1]), ple_w_proj[1].astype(BF16), ple_w_gate[1].astype(BF16))
    return x2.reshape(batch, seq, D_MODEL)
```
